```python
import math
import jax, jax.numpy as jnp
from jax import lax
import numpy as np

D_MODEL = 2048
BATCH = 4
SEQ = 8192
DEPTH = 4

W_A = 512
CONV_A = 3
W_B = 512
CONV_B = 31
W_C = 512
GMLP_CHUNK = 128
GMLP_GROUPS = 4
N_HEADS = 4
HEAD_DIM = 128
W_D = N_HEADS * HEAD_DIM
MOBA_BLOCK = 256
MOBA_TOPK = 3
Q_CHUNK = 64
N_BUCKETS = 32
REL_MAX_DIST = 2048
D_FF = 4 * D_MODEL
N_BRANCH = 4
EPS = 1e-6
NEG_INF = -1e30

OFF_A = 0
OFF_B = OFF_A + 3 * W_A
OFF_C = OFF_B + 2 * W_B
OFF_D = OFF_C + 2 * W_C
OFF_G = OFF_D + 3 * W_D
IN_COLS = OFF_G + N_BRANCH * D_MODEL

kernel_name = "hybrid_gated_conv_gmlp_moba_trunk"


def _rms_norm(x, g):
    xf = x.astype(jnp.float32)
    y = xf * lax.rsqrt(jnp.mean(xf * xf, axis=-1, keepdims=True) + EPS)
    return (y * g.astype(jnp.float32)).astype(x.dtype)


def _layer_norm(x, g, b):
    xf = x.astype(jnp.float32)
    mu = jnp.mean(xf, axis=-1, keepdims=True)
    xc = xf - mu
    y = xc * lax.rsqrt(jnp.mean(xc * xc, axis=-1, keepdims=True) + EPS)
    return (y * g.astype(jnp.float32) + b.astype(jnp.float32)).astype(x.dtype)


def _causal_depthwise_conv(x, w):
    k = w.shape[0]
    return lax.conv_general_dilated(
        x, w[:, None, :].astype(x.dtype), window_strides=(1,), padding=[(k - 1, 0)],
        dimension_numbers=("NWC", "WIO", "NWC"), feature_group_count=x.shape[-1])


def _rel_bucket(dist):
    n = jnp.maximum(dist, 0)
    max_exact = N_BUCKETS // 2
    nf = jnp.maximum(n, 1).astype(jnp.float32)
    large = max_exact + (jnp.log(nf / max_exact) / math.log(REL_MAX_DIST / max_exact)
                         * (N_BUCKETS - max_exact)).astype(jnp.int32)
    large = jnp.minimum(large, N_BUCKETS - 1)
    return jnp.where(n < max_exact, n, large)


def _moba_attention(q, k, v, rel_bias):
    bsz, seq = q.shape[0], q.shape[1]
    n_blk = -(-seq // MOBA_BLOCK)
    s_pad = n_blk * MOBA_BLOCK
    pad = ((0, 0), (0, s_pad - seq), (0, 0), (0, 0))
    q = jnp.pad(q, pad).transpose(0, 2, 1, 3)
    k = jnp.pad(k, pad).transpose(0, 2, 1, 3)
    v = jnp.pad(v, pad).transpose(0, 2, 1, 3)
    k_blk = k.reshape(bsz, N_HEADS, n_blk, MOBA_BLOCK, HEAD_DIM)
    v_blk = v.reshape(bsz, N_HEADS, n_blk, MOBA_BLOCK, HEAD_DIM)
    k_mean = jnp.mean(k_blk.astype(jnp.float32), axis=3)
    pos = jnp.arange(s_pad, dtype=jnp.int32)
    q_blk_id = pos // MOBA_BLOCK
    gate = jnp.einsum("bhsd,bhnd->bhsn", q.astype(jnp.float32), k_mean)
    past = jnp.arange(n_blk, dtype=jnp.int32)[None, :] < q_blk_id[:, None]
    gate = jnp.where(past, gate, NEG_INF)
    n_sel = min(MOBA_TOPK, n_blk)
    _, sel_idx = lax.top_k(gate, n_sel)
    sel_valid = sel_idx < q_blk_id[:, None]

    n_chunk = s_pad // Q_CHUNK
    b_idx = jnp.arange(bsz)[:, None, None, None]
    h_idx = jnp.arange(N_HEADS)[None, :, None, None]
    h5 = jnp.arange(N_HEADS)[None, :, None, None, None]
    offs = jnp.arange(MOBA_BLOCK, dtype=jnp.int32)
    scale = HEAD_DIM ** -0.5

    def chunk(c):
        start = c * Q_CHUNK
        q_c = lax.dynamic_slice_in_dim(q, start, Q_CHUNK, axis=2)
        idx_c = lax.dynamic_slice_in_dim(sel_idx, start, Q_CHUNK, axis=2)
        val_c = lax.dynamic_slice_in_dim(sel_valid, start, Q_CHUNK, axis=2)
        q_pos = start + jnp.arange(Q_CHUNK, dtype=jnp.int32)
        own = start // MOBA_BLOCK
        k_own = lax.dynamic_index_in_dim(k_blk, own, axis=2, keepdims=False)
        v_own = lax.dynamic_index_in_dim(v_blk, own, axis=2, keepdims=False)
        d_own = q_pos[:, None] - (own * MOBA_BLOCK + offs)[None, :]
        l_own = (jnp.einsum("bhqd,bhkd->bhqk", q_c, k_own).astype(jnp.float32) * scale
                 + rel_bias[_rel_bucket(d_own)].transpose(2, 0, 1).astype(jnp.float32))
        l_own = jnp.where(d_own >= 0, l_own, NEG_INF)
        k_g = k_blk[b_idx, h_idx, idx_c]
        v_g = v_blk[b_idx, h_idx, idx_c]
        d_sel = q_pos[None, None, :, None, None] - (idx_c[..., None] * MOBA_BLOCK + offs)
        l_sel = (jnp.einsum("bhqd,bhqnkd->bhqnk", q_c, k_g).astype(jnp.float32) * scale
                 + rel_bias[_rel_bucket(d_sel), h5].astype(jnp.float32))
        l_sel = jnp.where(val_c[..., None], l_sel, NEG_INF)
        logits = jnp.concatenate(
            [l_own, l_sel.reshape(bsz, N_HEADS, Q_CHUNK, n_sel * MOBA_BLOCK)], axis=-1)
        p = jax.nn.softmax(logits, axis=-1).astype(v.dtype)
        p_own = p[..., :MOBA_BLOCK]
        p_sel = p[..., MOBA_BLOCK:].reshape(bsz, N_HEADS, Q_CHUNK, n_sel, MOBA_BLOCK)
        return (jnp.einsum("bhqk,bhkd->bhqd", p_own, v_own)
                + jnp.einsum("bhqnk,bhqnkd->bhqd", p_sel, v_g))

    out = lax.map(chunk, jnp.arange(n_chunk, dtype=jnp.int32))
    out = out.transpose(1, 0, 3, 2, 4).reshape(bsz, s_pad, W_D)
    return out[:, :seq]


def setup_inputs(seed: int = 0) -> dict:
    key = jax.random.key(seed)
    ks = jax.random.split(key, 32)
    f32 = jnp.float32
    nrm = lambda k, shape, s: jax.random.normal(k, shape, f32) * s
    return {
        "x": nrm(ks[0], (BATCH, SEQ, D_MODEL), 1.0),
        "rel_bias": nrm(ks[1], (N_BUCKETS, N_HEADS), 0.5),
        "norm_mix_g": 1.0 + nrm(ks[2], (DEPTH, D_MODEL), 0.05),
        "w_in": nrm(ks[3], (DEPTH, D_MODEL, IN_COLS), D_MODEL ** -0.5),
        "conv_a_w": nrm(ks[4], (DEPTH, CONV_A, W_A), CONV_A ** -0.5),
        "w_out_a": nrm(ks[5], (DEPTH, W_A, D_MODEL), W_A ** -0.5),
        "conv_b_w": nrm(ks[6], (DEPTH, CONV_B, W_B), CONV_B ** -0.5),
        "conv_b_bias": nrm(ks[7], (DEPTH, W_B), 0.02),
        "ln_b_g": 1.0 + nrm(ks[8], (DEPTH, W_B), 0.05),
        "ln_b_b": nrm(ks[9], (DEPTH, W_B), 0.02),
        "w_out_b": nrm(ks[10], (DEPTH, W_B, D_MODEL), W_B ** -0.5),
        "ln_c_g": 1.0 + nrm(ks[11], (DEPTH, W_C), 0.05),
        "ln_c_b": nrm(ks[12], (DEPTH, W_C), 0.02),
        "w_spatial": nrm(ks[13], (DEPTH, GMLP_GROUPS, GMLP_CHUNK, GMLP_CHUNK), GMLP_CHUNK ** -0.5),
        "b_spatial": 1.0 + nrm(ks[14], (DEPTH, GMLP_GROUPS, GMLP_CHUNK), 0.1),
        "w_out_c": nrm(ks[15], (DEPTH, W_C, D_MODEL), W_C ** -0.5),
        "q_norm_g": 1.0 + nrm(ks[16], (DEPTH, HEAD_DIM), 0.05),
        "k_norm_g": 1.0 + nrm(ks[17], (DEPTH, HEAD_DIM), 0.05),
        "w_out_d": nrm(ks[18], (DEPTH, W_D, D_MODEL), W_D ** -0.5),
        "w_o": nrm(ks[19], (DEPTH, D_MODEL, D_MODEL), D_MODEL ** -0.5),
        "norm_mlp_g": 1.0 + nrm(ks[20], (DEPTH, D_MODEL), 0.05),
        "w_mlp_in": nrm(ks[21], (DEPTH, D_MODEL, D_FF), D_MODEL ** -0.5),
        "w_mlp_out": nrm(ks[22], (DEPTH, D_FF, D_MODEL), D_FF ** -0.5),
    }


def reference(x, rel_bias, norm_mix_g, w_in, conv_a_w, w_out_a, conv_b_w, conv_b_bias,
              ln_b_g, ln_b_b, w_out_b, ln_c_g, ln_c_b, w_spatial, b_spatial, w_out_c,
              q_norm_g, k_norm_g, w_out_d, w_o, norm_mlp_g, w_mlp_in, w_mlp_out):
    bsz, seq = x.shape[0], x.shape[1]
    n_chunks = seq // GMLP_CHUNK
    causal_tri = jnp.tril(jnp.ones((GMLP_CHUNK, GMLP_CHUNK), x.dtype))
    for l in range(DEPTH):
        h = _rms_norm(x, norm_mix_g[l])
        z = h @ w_in[l]

        a_b, a_c, a_x = jnp.split(z[..., OFF_A:OFF_B], 3, axis=-1)
        y_a = (a_b * _causal_depthwise_conv(a_c * a_x, conv_a_w[l])) @ w_out_a[l]

        b_a, b_g = jnp.split(z[..., OFF_B:OFF_C], 2, axis=-1)
        hb = _causal_depthwise_conv(b_a * jax.nn.sigmoid(b_g), conv_b_w[l]) + conv_b_bias[l]
        y_b = jax.nn.silu(_layer_norm(hb, ln_b_g[l], ln_b_b[l])) @ w_out_b[l]

        u, vv = jnp.split(jax.nn.gelu(z[..., OFF_C:OFF_D]), 2, axis=-1)
        vv = _layer_norm(vv, ln_c_g[l], ln_c_b[l])
        vv = vv.reshape(bsz, n_chunks, GMLP_CHUNK, GMLP_GROUPS, W_C // GMLP_GROUPS)
        sv = (jnp.einsum("gts,bnsgc->bntgc", w_spatial[l] * causal_tri, vv)
              + b_spatial[l].T[:, :, None])
        y_c = (u * sv.reshape(bsz, seq, W_C)) @ w_out_c[l]

        q, k, v = jnp.split(z[..., OFF_D:OFF_G], 3, axis=-1)
        q = _rms_norm(q.reshape(bsz, seq, N_HEADS, HEAD_DIM), q_norm_g[l])
        k = _rms_norm(k.reshape(bsz, seq, N_HEADS, HEAD_DIM), k_norm_g[l])
        v = v.reshape(bsz, seq, N_HEADS, HEAD_DIM)
        y_d = _moba_attention(q, k, v, rel_bias) @ w_out_d[l]

        g = jax.nn.sigmoid(z[..., OFF_G:].astype(jnp.float32)).astype(x.dtype)
        g = g.reshape(bsz, seq, N_BRANCH, D_MODEL)
        merged = g[:, :, 0] * y_a + g[:, :, 1] * y_b + g[:, :, 2] * y_c + g[:, :, 3] * y_d
        x = x + merged @ w_o[l]

        h2 = _rms_norm(x, norm_mlp_g[l])
        x = x + jnp.square(jax.nn.relu(h2 @ w_mlp_in[l])) @ w_mlp_out[l]
    return x
```

```python
import functools
import math

import jax
import jax.numpy as jnp
from jax import lax
from jax.experimental import pallas as pl
from jax.experimental.pallas import tpu as pltpu

F32 = jnp.float32
BF16 = jnp.bfloat16

W_A = 512
CONV_A = 3
W_B = 512
CONV_B = 31
W_C = 512
GMLP_CHUNK = 128
GMLP_GROUPS = 4
N_HEADS = 4
HEAD_DIM = 128
W_D = N_HEADS * HEAD_DIM
MOBA_BLOCK = 256
MOBA_TOPK = 3
N_BUCKETS = 32
REL_MAX_DIST = 2048
N_BRANCH = 4
EPS = 1e-6
NEG_INF = -1e30
BELOW_NEG_INF = -3e38

OFF_A = 0
OFF_B = OFF_A + 3 * W_A
OFF_C = OFF_B + 2 * W_B
OFF_D = OFF_C + 2 * W_C
OFF_G = OFF_D + 3 * W_D

LANES = 128
VMEM_LIMIT_BYTES = 56 * 1024 * 1024
HALO_A = 8
HALO_B = 32
CONV_ROWS = 32


def _cparams(*semantics):
    return pltpu.CompilerParams(dimension_semantics=semantics,
                                vmem_limit_bytes=VMEM_LIMIT_BYTES)


def _dot(a, b):
    return jnp.dot(a, b, preferred_element_type=F32)


def _dot_nt(a, b, precision=None):
    return lax.dot_general(a, b, (((1,), (1,)), ((), ())), precision=precision,
                           preferred_element_type=F32)


def _rms(x, g):
    return x * lax.rsqrt(jnp.mean(x * x, axis=-1, keepdims=True) + EPS) * g


def _layer_norm(x, g, b):
    mu = jnp.mean(x, axis=-1, keepdims=True)
    xc = x - mu
    return xc * lax.rsqrt(jnp.mean(xc * xc, axis=-1, keepdims=True) + EPS) * g + b


def _rmsnorm_to(x_ref, g_ref, h_ref, rows=128):
    def body(c, carry):
        r = pl.multiple_of(c * rows, rows)
        h_ref[pl.ds(r, rows), :] = _rms(x_ref[pl.ds(r, rows), :], g_ref[...]).astype(h_ref.dtype)
        return carry
    lax.fori_loop(0, x_ref.shape[0] // rows, body, 0)


def _inproj_kernel(x_ref, g_ref, w_ref, o_ref, h_ref):
    @pl.when(pl.program_id(1) == 0)
    def _():
        _rmsnorm_to(x_ref, g_ref, h_ref)
    o_ref[...] = _dot(h_ref[...], w_ref[...]).astype(o_ref.dtype)


def _inproj(x, g, w, col_lo, ncols, out_dtype, tm, tn):
    n, d = x.shape
    assert n % tm == 0 and ncols % tn == 0 and col_lo % tn == 0
    c0 = col_lo // tn
    return pl.pallas_call(
        _inproj_kernel,
        grid=(n // tm, ncols // tn),
        in_specs=[pl.BlockSpec((tm, d), lambda i, j: (i, 0)),
                  pl.BlockSpec((1, d), lambda i, j: (0, 0)),
                  pl.BlockSpec((d, tn), lambda i, j: (0, c0 + j))],
        out_specs=pl.BlockSpec((tm, tn), lambda i, j: (i, j)),
        out_shape=jax.ShapeDtypeStruct((n, ncols), out_dtype),
        scratch_shapes=[pltpu.VMEM((tm, d), BF16)],
        compiler_params=_cparams("arbitrary", "arbitrary"),
        name="inproj",
    )(x, g, w)


def _bias_tile_kernel(rb_ref, o_ref):
    h = pl.program_id(0)
    dlt = pl.program_id(1)
    r = lax.broadcasted_iota(jnp.int32, (MOBA_BLOCK, MOBA_BLOCK), 0)
    c = lax.broadcasted_iota(jnp.int32, (MOBA_BLOCK, MOBA_BLOCK), 1)
    d = dlt * MOBA_BLOCK + r - c
    n = jnp.maximum(d, 0)
    max_exact = N_BUCKETS // 2
    nf = jnp.maximum(n, 1).astype(F32)
    large = max_exact + (jnp.log(nf / max_exact) / math.log(REL_MAX_DIST / max_exact)
                         * (N_BUCKETS - max_exact)).astype(jnp.int32)
    large = jnp.minimum(large, N_BUCKETS - 1)
    bucket = jnp.where(n < max_exact, n, large)
    val = jnp.zeros((MOBA_BLOCK, MOBA_BLOCK), F32)
    for b in range(N_BUCKETS):
        val = jnp.where(bucket == b, rb_ref[b, h], val)
    o_ref[0, 0] = jnp.where(d >= 0, val, NEG_INF)


def _bias_tiles(rel_bias, n_blk):
    return pl.pallas_call(
        _bias_tile_kernel,
        grid=(N_HEADS, n_blk),
        in_specs=[pl.BlockSpec(memory_space=pltpu.SMEM)],
        out_specs=pl.BlockSpec((1, 1, MOBA_BLOCK, MOBA_BLOCK), lambda h, d: (h, d, 0, 0)),
        out_shape=jax.ShapeDtypeStruct((N_HEADS, n_blk, MOBA_BLOCK, MOBA_BLOCK), F32),
        compiler_params=_cparams("arbitrary", "arbitrary"),
        name="bias_tiles",
    )(rel_bias)


def _qkprep_kernel(q_ref, k_ref, v_ref, gq_ref, gk_ref, qa_ref, ka_ref, vb_ref, kmean_ref):
    i = pl.program_id(1)

    @pl.when(i == 0)
    def _():
        kmean_ref[...] = jnp.zeros_like(kmean_ref)

    col = lax.broadcasted_iota(jnp.int32, (MOBA_BLOCK, LANES), 1)
    past = col < i
    onehot = jnp.where(col == i, 1.0, 0.0).astype(BF16)
    scale = HEAD_DIM ** -0.5
    for h in range(N_HEADS):
        sl = slice(h * HEAD_DIM, (h + 1) * HEAD_DIM)
        lo = 2 * h * HEAD_DIM
        qn = _rms(q_ref[:, sl].astype(F32), gq_ref[...])
        kn = _rms(k_ref[:, sl].astype(F32), gk_ref[...])
        gate = _dot_nt(qn, kmean_ref[h], precision=lax.Precision.HIGHEST)
        g = jnp.where(past, gate, NEG_INF)
        sel = col < 0
        for _ in range(MOBA_TOPK):
            m = jnp.max(g, axis=-1, keepdims=True)
            idx = jnp.min(jnp.where(g == m, col, LANES), axis=-1, keepdims=True)
            hit = col == idx
            sel = jnp.logical_or(sel, hit)
            g = jnp.where(hit, BELOW_NEG_INF, g)
        mask = jnp.where(jnp.logical_and(sel, past), 0.0, NEG_INF)
        qa_ref[:, lo:lo + HEAD_DIM] = (qn * scale).astype(BF16)
        qa_ref[:, lo + HEAD_DIM:lo + 2 * HEAD_DIM] = mask.astype(BF16)
        ka_ref[:, lo:lo + HEAD_DIM] = kn.astype(BF16)
        ka_ref[:, lo + HEAD_DIM:lo + 2 * HEAD_DIM] = onehot
        kmean_ref[h, pl.ds(i, 1), :] = jnp.mean(kn, axis=0, keepdims=True)
    vb_ref[...] = v_ref[...].astype(BF16)


def _qkprep(zf, gq, gk, bsz, seq):
    n = bsz * seq
    n_blk = seq // MOBA_BLOCK
    assert n_blk <= LANES
    cq = OFF_D // W_D
    row = lambda b, i: b * n_blk + i
    return pl.pallas_call(
        _qkprep_kernel,
        grid=(bsz, n_blk),
        in_specs=[pl.BlockSpec((MOBA_BLOCK, W_D), lambda b, i: (row(b, i), cq)),
                  pl.BlockSpec((MOBA_BLOCK, W_D), lambda b, i: (row(b, i), cq + 1)),
                  pl.BlockSpec((MOBA_BLOCK, W_D), lambda b, i: (row(b, i), cq + 2)),
                  pl.BlockSpec((1, HEAD_DIM), lambda b, i: (0, 0)),
                  pl.BlockSpec((1, HEAD_DIM), lambda b, i: (0, 0))],
        out_specs=[pl.BlockSpec((MOBA_BLOCK, 2 * W_D), lambda b, i: (row(b, i), 0)),
                   pl.BlockSpec((MOBA_BLOCK, 2 * W_D), lambda b, i: (row(b, i), 0)),
                   pl.BlockSpec((MOBA_BLOCK, W_D), lambda b, i: (row(b, i), 0))],
        out_shape=[jax.ShapeDtypeStruct((n, 2 * W_D), BF16),
                   jax.ShapeDtypeStruct((n, 2 * W_D), BF16),
                   jax.ShapeDtypeStruct((n, W_D), BF16)],
        scratch_shapes=[pltpu.VMEM((N_HEADS, LANES, HEAD_DIM), F32)],
        compiler_params=_cparams("arbitrary", "arbitrary"),
        name="qkprep",
    )(zf, zf, zf, gq, gk)


def _attn_kernel(qa_ref, ka_ref, vb_ref, bias_ref, o_ref, m_ref, l_ref, acc_ref):
    i = pl.program_id(2)
    r0 = pl.multiple_of(i * MOBA_BLOCK, MOBA_BLOCK)
    s = _dot_nt(qa_ref[:, :HEAD_DIM], ka_ref[pl.ds(r0, MOBA_BLOCK), :HEAD_DIM]) + bias_ref[0, 0]
    m = jnp.max(s, axis=-1, keepdims=True)
    p = jnp.exp(s - m)
    m_ref[...] = m
    l_ref[...] = jnp.sum(p, axis=-1, keepdims=True)
    acc_ref[...] = _dot(p.astype(BF16), vb_ref[pl.ds(r0, MOBA_BLOCK), :])

    def body(j, carry):
        rj = pl.multiple_of(j * MOBA_BLOCK, MOBA_BLOCK)
        s = _dot_nt(qa_ref[...], ka_ref[pl.ds(rj, MOBA_BLOCK), :]) + bias_ref[0, i - j]
        m_old = m_ref[...]
        m_new = jnp.maximum(m_old, jnp.max(s, axis=-1, keepdims=True))
        alpha = jnp.exp(m_old - m_new)
        p = jnp.exp(s - m_new)
        m_ref[...] = m_new
        l_ref[...] = alpha * l_ref[...] + jnp.sum(p, axis=-1, keepdims=True)
        acc_ref[...] = alpha * acc_ref[...] + _dot(p.astype(BF16), vb_ref[pl.ds(rj, MOBA_BLOCK), :])
        return carry

    lax.fori_loop(0, i, body, 0)
    o_ref[...] = (acc_ref[...] / l_ref[...]).astype(o_ref.dtype)


def _attn(qa, ka, vb, bias_t, bsz, seq):
    n = bsz * seq
    n_blk = seq // MOBA_BLOCK
    return pl.pallas_call(
        _attn_kernel,
        grid=(N_HEADS, bsz, n_blk),
        in_specs=[pl.BlockSpec((MOBA_BLOCK, 2 * HEAD_DIM), lambda h, b, i: (b * n_blk + i, h)),
                  pl.BlockSpec((seq, 2 * HEAD_DIM), lambda h, b, i: (b, h)),
                  pl.BlockSpec((seq, HEAD_DIM), lambda h, b, i: (b, h)),
                  pl.BlockSpec((1, n_blk, MOBA_BLOCK, MOBA_BLOCK), lambda h, b, i: (h, 0, 0, 0))],
        out_specs=pl.BlockSpec((MOBA_BLOCK, HEAD_DIM), lambda h, b, i: (b * n_blk + i, h)),
        out_shape=jax.ShapeDtypeStruct((n, W_D), BF16),
        scratch_shapes=[pltpu.VMEM((MOBA_BLOCK, 1), F32), pltpu.VMEM((MOBA_BLOCK, 1), F32),
                        pltpu.VMEM((MOBA_BLOCK, HEAD_DIM), F32)],
        compiler_params=_cparams("arbitrary", "arbitrary", "arbitrary"),
        name="attn",
    )(qa, ka, vb, bias_t)


def _branch_kernel(ab_ref, ac_ref, ax_ref, ba_ref, bg_ref, cu_ref, cv_ref,
                   wa_ref, wb_ref, bb_ref, lbg_ref, lbb_ref, lcg_ref, lcb_ref, ws_ref, bs_ref,
                   pa_ref, pb_ref, pc_ref, exta_ref, extb_ref):
    tm = ab_ref.shape[0]

    @pl.when(pl.program_id(1) == 0)
    def _():
        exta_ref[0:HALO_A, :] = jnp.zeros((HALO_A, W_A), F32)
        extb_ref[0:HALO_B, :] = jnp.zeros((HALO_B, W_B), F32)

    exta_ref[HALO_A:HALO_A + tm, :] = ac_ref[...] * ax_ref[...]
    extb_ref[HALO_B:HALO_B + tm, :] = ba_ref[...] * jax.nn.sigmoid(bg_ref[...])

    for r in range(0, tm, CONV_ROWS):
        ya = None
        for k in range(CONV_A):
            term = wa_ref[k:k + 1, :] * exta_ref[r + HALO_A - (CONV_A - 1) + k:
                                                 r + HALO_A - (CONV_A - 1) + k + CONV_ROWS, :]
            ya = term if ya is None else ya + term
        pa_ref[r:r + CONV_ROWS, :] = (ab_ref[r:r + CONV_ROWS, :] * ya).astype(pa_ref.dtype)

        yb = None
        for k in range(CONV_B):
            term = wb_ref[k:k + 1, :] * extb_ref[r + HALO_B - (CONV_B - 1) + k:
                                                 r + HALO_B - (CONV_B - 1) + k + CONV_ROWS, :]
            yb = term if yb is None else yb + term
        hb = _layer_norm(yb + bb_ref[...], lbg_ref[...], lbb_ref[...])
        pb_ref[r:r + CONV_ROWS, :] = jax.nn.silu(hb).astype(pb_ref.dtype)

    exta_ref[0:HALO_A, :] = exta_ref[tm:tm + HALO_A, :]
    extb_ref[0:HALO_B, :] = extb_ref[tm:tm + HALO_B, :]

    t_idx = lax.broadcasted_iota(jnp.int32, (GMLP_CHUNK, GMLP_CHUNK), 0)
    s_idx = lax.broadcasted_iota(jnp.int32, (GMLP_CHUNK, GMLP_CHUNK), 1)
    wg = W_C // GMLP_GROUPS
    wm = [jnp.where(s_idx <= t_idx, ws_ref[g], 0.0).astype(BF16) for g in range(GMLP_GROUPS)]
    for r in range(0, tm, GMLP_CHUNK):
        rows = slice(r, r + GMLP_CHUNK)
        u = jax.nn.gelu(cu_ref[rows, :])
        vn = _layer_norm(jax.nn.gelu(cv_ref[rows, :]), lcg_ref[...], lcb_ref[...]).astype(BF16)
        for g in range(GMLP_GROUPS):
            cols = slice(g * wg, (g + 1) * wg)
            sv = _dot(wm[g], vn[:, cols]) + bs_ref[:, cols]
            pc_ref[rows, cols] = (u[:, cols] * sv).astype(pc_ref.dtype)


def _branch(zf, lw, bsz, seq, tm):
    n = bsz * seq
    assert seq % tm == 0 and tm % GMLP_CHUNK == 0 and tm % CONV_ROWS == 0
    nt = seq // tm
    zspec = lambda c: pl.BlockSpec((tm, W_A), lambda b, t: (b * nt + t, c))
    full = lambda a: pl.BlockSpec(a.shape, lambda b, t: (0,) * a.ndim)
    params = [lw["conv_a_w"], lw["conv_b_w"], lw["conv_b_bias"], lw["ln_b_g"], lw["ln_b_b"],
              lw["ln_c_g"], lw["ln_c_b"], lw["w_spatial"], lw["b_spatial2"]]
    ospec = pl.BlockSpec((tm, W_A), lambda b, t: (b * nt + t, 0))
    return pl.pallas_call(
        _branch_kernel,
        grid=(bsz, nt),
        in_specs=[zspec(c) for c in range(7)] + [full(a) for a in params],
        out_specs=[ospec, ospec, ospec],
        out_shape=[jax.ShapeDtypeStruct((n, W_A), BF16)] * 3,
        scratch_shapes=[pltpu.VMEM((tm + HALO_A, W_A), F32), pltpu.VMEM((tm + HALO_B, W_B), F32)],
        compiler_params=_cparams("arbitrary", "arbitrary"),
        name="branch",
    )(*([zf] * 7), *params)


def _merge_kernel(pa_ref, pb_ref, pc_ref, pd_ref, wa_ref, wb_ref, wc_ref, wd_ref,
                  ga_ref, gb_ref, gc_ref, gd_ref, o_ref):
    acc = None
    for p_ref, w_ref, g_ref in ((pa_ref, wa_ref, ga_ref), (pb_ref, wb_ref, gb_ref),
                                (pc_ref, wc_ref, gc_ref), (pd_ref, wd_ref, gd_ref)):
        y = jax.nn.sigmoid(g_ref[...].astype(F32)) * _dot(p_ref[...], w_ref[...])
        acc = y if acc is None else acc + y
    o_ref[...] = acc.astype(o_ref.dtype)


def _merge(ps, ws, zg, d_model, tm, tn):
    n = zg.shape[0]
    assert n % tm == 0 and d_model % tn == 0
    nj = d_model // tn
    pspec = pl.BlockSpec((tm, W_A), lambda i, j: (i, 0))
    wspec = pl.BlockSpec((W_A, tn), lambda i, j: (0, j))
    gspec = lambda k: pl.BlockSpec((tm, tn), lambda i, j: (i, k * nj + j))
    return pl.pallas_call(
        _merge_kernel,
        grid=(n // tm, nj),
        in_specs=[pspec] * 4 + [wspec] * 4 + [gspec(k) for k in range(N_BRANCH)],
        out_specs=pl.BlockSpec((tm, tn), lambda i, j: (i, j)),
        out_shape=jax.ShapeDtypeStruct((n, d_model), BF16),
        compiler_params=_cparams("arbitrary", "arbitrary"),
        name="merge",
    )(*ps, *ws, zg, zg, zg, zg)


def _oproj_kernel(x_ref, m_ref, w_ref, o_ref):
    o_ref[...] = x_ref[...] + _dot(m_ref[...], w_ref[...])


def _oproj(x, merged, w, tm, tn):
    n, d = x.shape
    assert n % tm == 0 and d % tn == 0
    return pl.pallas_call(
        _oproj_kernel,
        grid=(n // tm, d // tn),
        in_specs=[pl.BlockSpec((tm, tn), lambda i, j: (i, j)),
                  pl.BlockSpec((tm, d), lambda i, j: (i, 0)),
                  pl.BlockSpec((d, tn), lambda i, j: (0, j))],
        out_specs=pl.BlockSpec((tm, tn), lambda i, j: (i, j)),
        out_shape=jax.ShapeDtypeStruct((n, d), F32),
        compiler_params=_cparams("arbitrary", "arbitrary"),
        name="oproj",
    )(x, merged, w)


def _mlp_kernel(x_ref, g_ref, w1_ref, w2_ref, o_ref, h_ref):
    @pl.when(pl.program_id(1) == 0)
    def _():
        _rmsnorm_to(x_ref, g_ref, h_ref)
        o_ref[...] = x_ref[...]
    a = jnp.square(jnp.maximum(_dot(h_ref[...], w1_ref[...]), 0.0)).astype(BF16)
    o_ref[...] += _dot(a, w2_ref[...])


def _mlp(x, g, w1, w2, tm, tf):
    n, d = x.shape
    d_ff = w1.shape[1]
    assert n % tm == 0 and d_ff % tf == 0
    return pl.pallas_call(
        _mlp_kernel,
        grid=(n // tm, d_ff // tf),
        in_specs=[pl.BlockSpec((tm, d), lambda i, f: (i, 0)),
                  pl.BlockSpec((1, d), lambda i, f: (0, 0)),
                  pl.BlockSpec((d, tf), lambda i, f: (0, f)),
                  pl.BlockSpec((tf, d), lambda i, f: (f, 0))],
        out_specs=pl.BlockSpec((tm, d), lambda i, f: (i, 0)),
        out_shape=jax.ShapeDtypeStruct((n, d), F32),
        scratch_shapes=[pltpu.VMEM((tm, d), BF16)],
        compiler_params=_cparams("arbitrary", "arbitrary"),
        name="mlp",
    )(x, g, w1, w2)


def _tiles(n, seq):
    return dict(tm_mm=min(1024, n), tm_branch=min(512, seq))


def kernel(x, rel_bias, norm_mix_g, w_in, conv_a_w, w_out_a, conv_b_w, conv_b_bias, ln_b_g, ln_b_b,
           w_out_b, ln_c_g, ln_c_b, w_spatial, b_spatial, w_out_c, q_norm_g, k_norm_g, w_out_d, w_o,
           norm_mlp_g, w_mlp_in, w_mlp_out):
    bsz, seq, d_model = x.shape
    n = bsz * seq
    assert seq % MOBA_BLOCK == 0 and w_in.shape[-1] == OFF_G + N_BRANCH * d_model
    t = _tiles(n, seq)
    tm = t["tm_mm"]
    row = lambda a: a[:, None, :]

    layers = dict(
        norm_mix_g=row(norm_mix_g), w_in=w_in.astype(BF16),
        conv_a_w=conv_a_w, conv_b_w=conv_b_w, conv_b_bias=row(conv_b_bias),
        ln_b_g=row(ln_b_g), ln_b_b=row(ln_b_b), ln_c_g=row(ln_c_g), ln_c_b=row(ln_c_b),
        w_spatial=w_spatial,
        b_spatial2=jnp.repeat(jnp.swapaxes(b_spatial, 1, 2), W_C // GMLP_GROUPS, axis=2),
        q_norm_g=row(q_norm_g), k_norm_g=row(k_norm_g),
        w_out_a=w_out_a.astype(BF16), w_out_b=w_out_b.astype(BF16),
        w_out_c=w_out_c.astype(BF16), w_out_d=w_out_d.astype(BF16),
        w_o=w_o.astype(BF16), norm_mlp_g=row(norm_mlp_g),
        w_mlp_in=w_mlp_in.astype(BF16), w_mlp_out=w_mlp_out.astype(BF16),
    )
    bias_t = _bias_tiles(rel_bias, seq // MOBA_BLOCK)

    def layer(xc, lw):
        zf = _inproj(xc, lw["norm_mix_g"], lw["w_in"], 0, OFF_G, F32, tm, 1024)
        zg = _inproj(xc, lw["norm_mix_g"], lw["w_in"], OFF_G, N_BRANCH * d_model, BF16, tm, 1024)
        qa, ka, vb = _qkprep(zf, lw["q_norm_g"], lw["k_norm_g"], bsz, seq)
        pd = _attn(qa, ka, vb, bias_t, bsz, seq)
        pa, pb, pc = _branch(zf, lw, bsz, seq, t["tm_branch"])
        merged = _merge((pa, pb, pc, pd),
                        (lw["w_out_a"], lw["w_out_b"], lw["w_out_c"], lw["w_out_d"]),
                        zg, d_model, tm, 512)
        xc = _oproj(xc, merged, lw["w_o"], tm, 1024)
        xc = _mlp(xc, lw["norm_mlp_g"], lw["w_mlp_in"], lw["w_mlp_out"], tm, 512)
        return xc, None

    out, _ = lax.scan(layer, x.reshape(n, d_model), layers)
    return out.reshape(bsz, seq, d_model)
```

```python
import functools
import math

import jax
import jax.numpy as jnp
from jax import lax
from jax.experimental import pallas as pl
from jax.experimental.pallas import tpu as pltpu

F32 = jnp.float32
BF16 = jnp.bfloat16

W_A = 512
CONV_A = 3
W_B = 512
CONV_B = 31
W_C = 512
GMLP_CHUNK = 128
GMLP_GROUPS = 4
N_HEADS = 4
HEAD_DIM = 128
W_D = N_HEADS * HEAD_DIM
MOBA_BLOCK = 256
MOBA_TOPK = 3
N_BUCKETS = 32
REL_MAX_DIST = 2048
N_BRANCH = 4
EPS = 1e-6
NEG_INF = -1e30
BELOW_NEG_INF = -3e38

OFF_A = 0
OFF_B = OFF_A + 3 * W_A
OFF_C = OFF_B + 2 * W_B
OFF_D = OFF_C + 2 * W_C
OFF_G = OFF_D + 3 * W_D

LANES = 128
VMEM_LIMIT_BYTES = 56 * 1024 * 1024
HALO_A = 8
HALO_B = 32
CONV_ROWS = 32
KV_BLOCKS = 4


def _cparams(*semantics):
    return pltpu.CompilerParams(dimension_semantics=semantics,
                                vmem_limit_bytes=VMEM_LIMIT_BYTES)


def _dot(a, b):
    return jnp.dot(a, b, preferred_element_type=F32)


def _dot_nt(a, b, precision=None):
    return lax.dot_general(a, b, (((1,), (1,)), ((), ())), precision=precision,
                           preferred_element_type=F32)


def _rms(x, g):
    return x * lax.rsqrt(jnp.mean(x * x, axis=-1, keepdims=True) + EPS) * g


def _layer_norm(x, g, b):
    mu = jnp.mean(x, axis=-1, keepdims=True)
    xc = x - mu
    return xc * lax.rsqrt(jnp.mean(xc * xc, axis=-1, keepdims=True) + EPS) * g + b


def _rmsnorm_to(x_ref, g_ref, h_ref, rows=128):
    def body(c, carry):
        r = pl.multiple_of(c * rows, rows)
        h_ref[pl.ds(r, rows), :] = _rms(x_ref[pl.ds(r, rows), :], g_ref[...]).astype(h_ref.dtype)
        return carry
    lax.fori_loop(0, x_ref.shape[0] // rows, body, 0)


def _inproj_kernel(x_ref, g_ref, w_ref, o_ref, h_ref):
    @pl.when(pl.program_id(1) == 0)
    def _():
        _rmsnorm_to(x_ref, g_ref, h_ref)
    o_ref[...] = _dot(h_ref[...], w_ref[...]).astype(o_ref.dtype)


def _inproj(x, g, w, col_lo, ncols, out_dtype, tm, tn):
    n, d = x.shape
    assert n % tm == 0 and ncols % tn == 0 and col_lo % tn == 0
    c0 = col_lo // tn
    return pl.pallas_call(
        _inproj_kernel,
        grid=(n // tm, ncols // tn),
        in_specs=[pl.BlockSpec((tm, d), lambda i, j: (i, 0)),
                  pl.BlockSpec((1, d), lambda i, j: (0, 0)),
                  pl.BlockSpec((d, tn), lambda i, j: (0, c0 + j))],
        out_specs=pl.BlockSpec((tm, tn), lambda i, j: (i, j)),
        out_shape=jax.ShapeDtypeStruct((n, ncols), out_dtype),
        scratch_shapes=[pltpu.VMEM((tm, d), BF16)],
        compiler_params=_cparams("arbitrary", "arbitrary"),
        name="inproj",
    )(x, g, w)


def _bias_tile_kernel(rb_ref, o_ref):
    h = pl.program_id(0)
    dlt = pl.program_id(1)
    c = lax.broadcasted_iota(jnp.int32, (MOBA_BLOCK, MOBA_BLOCK), 0)
    r = lax.broadcasted_iota(jnp.int32, (MOBA_BLOCK, MOBA_BLOCK), 1)
    d = dlt * MOBA_BLOCK + r - c
    n = jnp.maximum(d, 0)
    max_exact = N_BUCKETS // 2
    nf = jnp.maximum(n, 1).astype(F32)
    large = max_exact + (jnp.log(nf / max_exact) / math.log(REL_MAX_DIST / max_exact)
                         * (N_BUCKETS - max_exact)).astype(jnp.int32)
    large = jnp.minimum(large, N_BUCKETS - 1)
    bucket = jnp.where(n < max_exact, n, large)
    val = jnp.zeros((MOBA_BLOCK, MOBA_BLOCK), F32)
    for b in range(N_BUCKETS):
        val = jnp.where(bucket == b, rb_ref[b, h], val)
    o_ref[0, 0] = jnp.where(d >= 0, val, NEG_INF)


def _bias_tiles(rel_bias, n_blk):
    return pl.pallas_call(
        _bias_tile_kernel,
        grid=(N_HEADS, n_blk),
        in_specs=[pl.BlockSpec(memory_space=pltpu.SMEM)],
        out_specs=pl.BlockSpec((1, 1, MOBA_BLOCK, MOBA_BLOCK), lambda h, d: (h, d, 0, 0)),
        out_shape=jax.ShapeDtypeStruct((N_HEADS, n_blk, MOBA_BLOCK, MOBA_BLOCK), F32),
        compiler_params=_cparams("arbitrary", "arbitrary"),
        name="bias_tiles",
    )(rel_bias)


def _qkprep_kernel(q_ref, k_ref, v_ref, gq_ref, gk_ref, qat_ref, ka_ref, vt_ref, kmean_ref):
    i = pl.program_id(1)

    @pl.when(i == 0)
    def _():
        kmean_ref[...] = jnp.zeros_like(kmean_ref)

    col = lax.broadcasted_iota(jnp.int32, (MOBA_BLOCK, LANES), 1)
    past = col < i
    onehot = jnp.where(col == i, 1.0, 0.0).astype(BF16)
    scale = HEAD_DIM ** -0.5
    for h in range(N_HEADS):
        sl = slice(h * HEAD_DIM, (h + 1) * HEAD_DIM)
        lo = 2 * h * HEAD_DIM
        qn = _rms(q_ref[:, sl].astype(F32), gq_ref[...])
        kn = _rms(k_ref[:, sl].astype(F32), gk_ref[...])
        gate = _dot_nt(qn, kmean_ref[h], precision=lax.Precision.HIGHEST)
        g = jnp.where(past, gate, NEG_INF)
        sel = col < 0
        for _ in range(MOBA_TOPK):
            m = jnp.max(g, axis=-1, keepdims=True)
            idx = jnp.min(jnp.where(g == m, col, LANES), axis=-1, keepdims=True)
            hit = col == idx
            sel = jnp.logical_or(sel, hit)
            g = jnp.where(hit, BELOW_NEG_INF, g)
        keep = jnp.logical_or(jnp.logical_and(sel, past), col == i)
        mask = jnp.where(keep, 0.0, NEG_INF)
        qat_ref[lo:lo + HEAD_DIM, :] = (qn * scale).T.astype(BF16)
        qat_ref[lo + HEAD_DIM:lo + 2 * HEAD_DIM, :] = mask.T.astype(BF16)
        ka_ref[:, lo:lo + HEAD_DIM] = kn.astype(BF16)
        ka_ref[:, lo + HEAD_DIM:lo + 2 * HEAD_DIM] = onehot
        kmean_ref[h, pl.ds(i, 1), :] = jnp.mean(kn, axis=0, keepdims=True)
        vt_ref[sl, :] = v_ref[:, sl].astype(F32).T.astype(BF16)


def _qkprep(zf, gq, gk, bsz, seq):
    n = bsz * seq
    n_blk = seq // MOBA_BLOCK
    assert n_blk <= LANES
    cq = OFF_D // W_D
    row = lambda b, i: b * n_blk + i
    return pl.pallas_call(
        _qkprep_kernel,
        grid=(bsz, n_blk),
        in_specs=[pl.BlockSpec((MOBA_BLOCK, W_D), lambda b, i: (row(b, i), cq)),
                  pl.BlockSpec((MOBA_BLOCK, W_D), lambda b, i: (row(b, i), cq + 1)),
                  pl.BlockSpec((MOBA_BLOCK, W_D), lambda b, i: (row(b, i), cq + 2)),
                  pl.BlockSpec((1, HEAD_DIM), lambda b, i: (0, 0)),
                  pl.BlockSpec((1, HEAD_DIM), lambda b, i: (0, 0))],
        out_specs=[pl.BlockSpec((2 * W_D, MOBA_BLOCK), lambda b, i: (0, row(b, i))),
                   pl.BlockSpec((MOBA_BLOCK, 2 * W_D), lambda b, i: (row(b, i), 0)),
                   pl.BlockSpec((W_D, MOBA_BLOCK), lambda b, i: (0, row(b, i)))],
        out_shape=[jax.ShapeDtypeStruct((2 * W_D, n), BF16),
                   jax.ShapeDtypeStruct((n, 2 * W_D), BF16),
                   jax.ShapeDtypeStruct((W_D, n), BF16)],
        scratch_shapes=[pltpu.VMEM((N_HEADS, LANES, HEAD_DIM), F32)],
        compiler_params=_cparams("arbitrary", "arbitrary"),
        name="qkprep",
    )(zf, zf, zf, gq, gk)


def _attn_kernel(qat_ref, ka_ref, vt_ref, bias_ref, o_ref, s_ref, m_ref, l_ref, acc_ref):
    i = pl.program_id(2)
    n_groups = i // KV_BLOCKS + 1
    kv = KV_BLOCKS * MOBA_BLOCK
    m_ref[...] = jnp.full(m_ref.shape, NEG_INF, F32)
    l_ref[...] = jnp.zeros(l_ref.shape, F32)
    acc_ref[...] = jnp.zeros(acc_ref.shape, F32)

    def scores(g, carry):
        r0 = pl.multiple_of(g * kv, kv)
        s = _dot(ka_ref[pl.ds(r0, kv), :], qat_ref[...])
        mx = m_ref[...]
        for jj in range(KV_BLOCKS):
            dlt = jnp.maximum(i - (g * KV_BLOCKS + jj), 0)
            part = s[jj * MOBA_BLOCK:(jj + 1) * MOBA_BLOCK, :] + bias_ref[0, dlt]
            s_ref[pl.ds(pl.multiple_of(r0 + jj * MOBA_BLOCK, MOBA_BLOCK), MOBA_BLOCK), :] = part
            mx = jnp.maximum(mx, jnp.max(part, axis=0, keepdims=True))
        m_ref[...] = mx
        return carry

    lax.fori_loop(0, n_groups, scores, 0)

    def values(g, carry):
        r0 = pl.multiple_of(g * kv, kv)
        m = m_ref[...]
        lsum = l_ref[...]
        pv = acc_ref[...]
        for jj in range(KV_BLOCKS):
            rj = pl.multiple_of(r0 + jj * MOBA_BLOCK, MOBA_BLOCK)
            p = jnp.exp(s_ref[pl.ds(rj, MOBA_BLOCK), :] - m)
            lsum = lsum + jnp.sum(p, axis=0, keepdims=True)
            pv = pv + _dot(vt_ref[:, pl.ds(rj, MOBA_BLOCK)], p.astype(BF16))
        l_ref[...] = lsum
        acc_ref[...] = pv
        return carry

    lax.fori_loop(0, n_groups, values, 0)
    o_ref[...] = (acc_ref[...] / l_ref[...]).T.astype(o_ref.dtype)


def _attn(qat, ka, vt, bias_t, bsz, seq):
    n = bsz * seq
    n_blk = seq // MOBA_BLOCK
    assert n_blk % KV_BLOCKS == 0
    return pl.pallas_call(
        _attn_kernel,
        grid=(N_HEADS, bsz, n_blk),
        in_specs=[pl.BlockSpec((2 * HEAD_DIM, MOBA_BLOCK), lambda h, b, i: (h, b * n_blk + i)),
                  pl.BlockSpec((seq, 2 * HEAD_DIM), lambda h, b, i: (b, h)),
                  pl.BlockSpec((HEAD_DIM, seq), lambda h, b, i: (h, b)),
                  pl.BlockSpec((1, n_blk, MOBA_BLOCK, MOBA_BLOCK), lambda h, b, i: (h, 0, 0, 0))],
        out_specs=pl.BlockSpec((MOBA_BLOCK, HEAD_DIM), lambda h, b, i: (b * n_blk + i, h)),
        out_shape=jax.ShapeDtypeStruct((n, W_D), BF16),
        scratch_shapes=[pltpu.VMEM((seq, MOBA_BLOCK), F32),
                        pltpu.VMEM((1, MOBA_BLOCK), F32), pltpu.VMEM((1, MOBA_BLOCK), F32),
                        pltpu.VMEM((HEAD_DIM, MOBA_BLOCK), F32)],
        compiler_params=_cparams("arbitrary", "arbitrary", "arbitrary"),
        name="attn",
    )(qat, ka, vt, bias_t)


def _branch_kernel(ab_ref, ac_ref, ax_ref, ba_ref, bg_ref, cu_ref, cv_ref,
                   wa_ref, wb_ref, bb_ref, lbg_ref, lbb_ref, lcg_ref, lcb_ref, ws_ref, bs_ref,
                   pa_ref, pb_ref, pc_ref, exta_ref, extb_ref):
    tm = ab_ref.shape[0]

    @pl.when(pl.program_id(1) == 0)
    def _():
        exta_ref[0:HALO_A, :] = jnp.zeros((HALO_A, W_A), F32)
        extb_ref[0:HALO_B, :] = jnp.zeros((HALO_B, W_B), F32)

    exta_ref[HALO_A:HALO_A + tm, :] = ac_ref[...] * ax_ref[...]
    extb_ref[HALO_B:HALO_B + tm, :] = ba_ref[...] * jax.nn.sigmoid(bg_ref[...])

    for r in range(0, tm, CONV_ROWS):
        ya = None
        for k in range(CONV_A):
            term = wa_ref[k:k + 1, :] * exta_ref[r + HALO_A - (CONV_A - 1) + k:
                                                 r + HALO_A - (CONV_A - 1) + k + CONV_ROWS, :]
            ya = term if ya is None else ya + term
        pa_ref[r:r + CONV_ROWS, :] = (ab_ref[r:r + CONV_ROWS, :] * ya).astype(pa_ref.dtype)

        yb = None
        for k in range(CONV_B):
            term = wb_ref[k:k + 1, :] * extb_ref[r + HALO_B - (CONV_B - 1) + k:
                                                 r + HALO_B - (CONV_B - 1) + k + CONV_ROWS, :]
            yb = term if yb is None else yb + term
        hb = _layer_norm(yb + bb_ref[...], lbg_ref[...], lbb_ref[...])
        pb_ref[r:r + CONV_ROWS, :] = jax.nn.silu(hb).astype(pb_ref.dtype)

    exta_ref[0:HALO_A, :] = exta_ref[tm:tm + HALO_A, :]
    extb_ref[0:HALO_B, :] = extb_ref[tm:tm + HALO_B, :]

    t_idx = lax.broadcasted_iota(jnp.int32, (GMLP_CHUNK, GMLP_CHUNK), 0)
    s_idx = lax.broadcasted_iota(jnp.int32, (GMLP_CHUNK, GMLP_CHUNK), 1)
    wg = W_C // GMLP_GROUPS
    wm = [jnp.where(s_idx <= t_idx, ws_ref[g], 0.0).astype(BF16) for g in range(GMLP_GROUPS)]
    for r in range(0, tm, GMLP_CHUNK):
        rows = slice(r, r + GMLP_CHUNK)
        u = jax.nn.gelu(cu_ref[rows, :])
        vn = _layer_norm(jax.nn.gelu(cv_ref[rows, :]), lcg_ref[...], lcb_ref[...]).astype(BF16)
        for g in range(GMLP_GROUPS):
            cols = slice(g * wg, (g + 1) * wg)
            sv = _dot(wm[g], vn[:, cols]) + bs_ref[:, cols]
            pc_ref[rows, cols] = (u[:, cols] * sv).astype(pc_ref.dtype)


def _branch(zf, lw, bsz, seq, tm):
    n = bsz * seq
    assert seq % tm == 0 and tm % GMLP_CHUNK == 0 and tm % CONV_ROWS == 0
    nt = seq // tm
    zspec = lambda c: pl.BlockSpec((tm, W_A), lambda b, t: (b * nt + t, c))
    full = lambda a: pl.BlockSpec(a.shape, lambda b, t: (0,) * a.ndim)
    params = [lw["conv_a_w"], lw["conv_b_w"], lw["conv_b_bias"], lw["ln_b_g"], lw["ln_b_b"],
              lw["ln_c_g"], lw["ln_c_b"], lw["w_spatial"], lw["b_spatial2"]]
    ospec = pl.BlockSpec((tm, W_A), lambda b, t: (b * nt + t, 0))
    return pl.pallas_call(
        _branch_kernel,
        grid=(bsz, nt),
        in_specs=[zspec(c) for c in range(7)] + [full(a) for a in params],
        out_specs=[ospec, ospec, ospec],
        out_shape=[jax.ShapeDtypeStruct((n, W_A), BF16)] * 3,
        scratch_shapes=[pltpu.VMEM((tm + HALO_A, W_A), F32), pltpu.VMEM((tm + HALO_B, W_B), F32)],
        compiler_params=_cparams("arbitrary", "arbitrary"),
        name="branch",
    )(*([zf] * 7), *params)


def _merge_kernel(pa_ref, pb_ref, pc_ref, pd_ref, wa_ref, wb_ref, wc_ref, wd_ref,
                  ga_ref, gb_ref, gc_ref, gd_ref, o_ref):
    acc = None
    for p_ref, w_ref, g_ref in ((pa_ref, wa_ref, ga_ref), (pb_ref, wb_ref, gb_ref),
                                (pc_ref, wc_ref, gc_ref), (pd_ref, wd_ref, gd_ref)):
        y = jax.nn.sigmoid(g_ref[...].astype(F32)) * _dot(p_ref[...], w_ref[...])
        acc = y if acc is None else acc + y
    o_ref[...] = acc.astype(o_ref.dtype)


def _merge(ps, ws, zg, d_model, tm, tn):
    n = zg.shape[0]
    assert n % tm == 0 and d_model % tn == 0
    nj = d_model // tn
    pspec = pl.BlockSpec((tm, W_A), lambda i, j: (i, 0))
    wspec = pl.BlockSpec((W_A, tn), lambda i, j: (0, j))
    gspec = lambda k: pl.BlockSpec((tm, tn), lambda i, j: (i, k * nj + j))
    return pl.pallas_call(
        _merge_kernel,
        grid=(n // tm, nj),
        in_specs=[pspec] * 4 + [wspec] * 4 + [gspec(k) for k in range(N_BRANCH)],
        out_specs=pl.BlockSpec((tm, tn), lambda i, j: (i, j)),
        out_shape=jax.ShapeDtypeStruct((n, d_model), BF16),
        compiler_params=_cparams("arbitrary", "arbitrary"),
        name="merge",
    )(*ps, *ws, zg, zg, zg, zg)


def _oproj_kernel(x_ref, m_ref, w_ref, o_ref):
    o_ref[...] = x_ref[...] + _dot(m_ref[...], w_ref[...])


def _oproj(x, merged, w, tm, tn):
    n, d = x.shape
    assert n % tm == 0 and d % tn == 0
    return pl.pallas_call(
        _oproj_kernel,
        grid=(n // tm, d // tn),
        in_specs=[pl.BlockSpec((tm, tn), lambda i, j: (i, j)),
                  pl.BlockSpec((tm, d), lambda i, j: (i, 0)),
                  pl.BlockSpec((d, tn), lambda i, j: (0, j))],
        out_specs=pl.BlockSpec((tm, tn), lambda i, j: (i, j)),
        out_shape=jax.ShapeDtypeStruct((n, d), F32),
        compiler_params=_cparams("arbitrary", "arbitrary"),
        name="oproj",
    )(x, merged, w)


def _mlp_kernel(x_ref, g_ref, w1_ref, w2_ref, o_ref, h_ref):
    @pl.when(pl.program_id(1) == 0)
    def _():
        _rmsnorm_to(x_ref, g_ref, h_ref)
        o_ref[...] = x_ref[...]
    a = jnp.square(jnp.maximum(_dot(h_ref[...], w1_ref[...]), 0.0)).astype(BF16)
    o_ref[...] += _dot(a, w2_ref[...])


def _mlp(x, g, w1, w2, tm, tf):
    n, d = x.shape
    d_ff = w1.shape[1]
    assert n % tm == 0 and d_ff % tf == 0
    return pl.pallas_call(
        _mlp_kernel,
        grid=(n // tm, d_ff // tf),
        in_specs=[pl.BlockSpec((tm, d), lambda i, f: (i, 0)),
                  pl.BlockSpec((1, d), lambda i, f: (0, 0)),
                  pl.BlockSpec((d, tf), lambda i, f: (0, f)),
                  pl.BlockSpec((tf, d), lambda i, f: (f, 0))],
        out_specs=pl.BlockSpec((tm, d), lambda i, f: (i, 0)),
        out_shape=jax.ShapeDtypeStruct((n, d), F32),
        scratch_shapes=[pltpu.VMEM((tm, d), BF16)],
        compiler_params=_cparams("arbitrary", "arbitrary"),
        name="mlp",
    )(x, g, w1, w2)


def _tiles(n, seq):
    return dict(tm_mm=min(1024, n), tm_branch=min(512, seq))


def kernel(x, rel_bias, norm_mix_g, w_in, conv_a_w, w_out_a, conv_b_w, conv_b_bias, ln_b_g, ln_b_b,
           w_out_b, ln_c_g, ln_c_b, w_spatial, b_spatial, w_out_c, q_norm_g, k_norm_g, w_out_d, w_o,
           norm_mlp_g, w_mlp_in, w_mlp_out):
    bsz, seq, d_model = x.shape
    n = bsz * seq
    assert seq % MOBA_BLOCK == 0 and w_in.shape[-1] == OFF_G + N_BRANCH * d_model
    t = _tiles(n, seq)
    tm = t["tm_mm"]
    row = lambda a: a[:, None, :]

    layers = dict(
        norm_mix_g=row(norm_mix_g), w_in=w_in.astype(BF16),
        conv_a_w=conv_a_w, conv_b_w=conv_b_w, conv_b_bias=row(conv_b_bias),
        ln_b_g=row(ln_b_g), ln_b_b=row(ln_b_b), ln_c_g=row(ln_c_g), ln_c_b=row(ln_c_b),
        w_spatial=w_spatial,
        b_spatial2=jnp.repeat(jnp.swapaxes(b_spatial, 1, 2), W_C // GMLP_GROUPS, axis=2),
        q_norm_g=row(q_norm_g), k_norm_g=row(k_norm_g),
        w_out_a=w_out_a.astype(BF16), w_out_b=w_out_b.astype(BF16),
        w_out_c=w_out_c.astype(BF16), w_out_d=w_out_d.astype(BF16),
        w_o=w_o.astype(BF16), norm_mlp_g=row(norm_mlp_g),
        w_mlp_in=w_mlp_in.astype(BF16), w_mlp_out=w_mlp_out.astype(BF16),
    )
    bias_t = _bias_tiles(rel_bias, seq // MOBA_BLOCK)

    def layer(xc, lw):
        zf = _inproj(xc, lw["norm_mix_g"], lw["w_in"], 0, OFF_G, F32, tm, 1024)
        zg = _inproj(xc, lw["norm_mix_g"], lw["w_in"], OFF_G, N_BRANCH * d_model, BF16, tm, 1024)
        qat, ka, vt = _qkprep(zf, lw["q_norm_g"], lw["k_norm_g"], bsz, seq)
        pd = _attn(qat, ka, vt, bias_t, bsz, seq)
        pa, pb, pc = _branch(zf, lw, bsz, seq, t["tm_branch"])
        merged = _merge((pa, pb, pc, pd),
                        (lw["w_out_a"], lw["w_out_b"], lw["w_out_c"], lw["w_out_d"]),
                        zg, d_model, tm, 512)
        xc = _oproj(xc, merged, lw["w_o"], tm, 1024)
        xc = _mlp(xc, lw["norm_mlp_g"], lw["w_mlp_in"], lw["w_mlp_out"], tm, 512)
        return xc, None

    out, _ = lax.scan(layer, x.reshape(n, d_model), layers)
    return out.reshape(bsz, seq, d_model)
```

```python
import functools
import math

import jax
import jax.numpy as jnp
from jax import lax
from jax.experimental import pallas as pl
from jax.experimental.pallas import tpu as pltpu

F32 = jnp.float32
BF16 = jnp.bfloat16

W_A = 512
CONV_A = 3
W_B = 512
CONV_B = 31
W_C = 512
GMLP_CHUNK = 128
GMLP_GROUPS = 4
N_HEADS = 4
HEAD_DIM = 128
W_D = N_HEADS * HEAD_DIM
MOBA_BLOCK = 256
MOBA_TOPK = 3
N_BUCKETS = 32
REL_MAX_DIST = 2048
N_BRANCH = 4
EPS = 1e-6
NEG_INF = -1e30
BELOW_NEG_INF = -3e38

OFF_A = 0
OFF_B = OFF_A + 3 * W_A
OFF_C = OFF_B + 2 * W_B
OFF_D = OFF_C + 2 * W_C
OFF_G = OFF_D + 3 * W_D

LANES = 128
SUBLANES = 8
VMEM_LIMIT_BYTES = 56 * 1024 * 1024
HALO_A = 8
HALO_B = 32
CONV_ROWS = 32
KV_BLOCKS = 8
Q_BLOCKS = 2
V_ROWS = HEAD_DIM + 16
LOG2E = math.log2(math.e)


def _cparams(*semantics):
    return pltpu.CompilerParams(dimension_semantics=semantics,
                                vmem_limit_bytes=VMEM_LIMIT_BYTES)


def _dot(a, b):
    return jnp.dot(a, b, preferred_element_type=F32)


def _dot_nt(a, b, precision=None):
    return lax.dot_general(a, b, (((1,), (1,)), ((), ())), precision=precision,
                           preferred_element_type=F32)


def _rms(x, g):
    return x * lax.rsqrt(jnp.mean(x * x, axis=-1, keepdims=True) + EPS) * g


def _layer_norm(x, g, b):
    mu = jnp.mean(x, axis=-1, keepdims=True)
    xc = x - mu
    return xc * lax.rsqrt(jnp.mean(xc * xc, axis=-1, keepdims=True) + EPS) * g + b


def _rmsnorm_to(x_ref, g_ref, h_ref, rows=128):
    def body(c, carry):
        r = pl.multiple_of(c * rows, rows)
        h_ref[pl.ds(r, rows), :] = _rms(x_ref[pl.ds(r, rows), :], g_ref[...]).astype(h_ref.dtype)
        return carry
    lax.fori_loop(0, x_ref.shape[0] // rows, body, 0)


def _inproj_kernel(x_ref, g_ref, w_ref, o_ref, h_ref):
    @pl.when(pl.program_id(1) == 0)
    def _():
        _rmsnorm_to(x_ref, g_ref, h_ref)
    o_ref[...] = _dot(h_ref[...], w_ref[...]).astype(o_ref.dtype)


def _inproj(x, g, w, col_lo, ncols, out_dtype, tm, tn):
    n, d = x.shape
    assert n % tm == 0 and ncols % tn == 0 and col_lo % tn == 0
    c0 = col_lo // tn
    return pl.pallas_call(
        _inproj_kernel,
        grid=(n // tm, ncols // tn),
        in_specs=[pl.BlockSpec((tm, d), lambda i, j: (i, 0)),
                  pl.BlockSpec((1, d), lambda i, j: (0, 0)),
                  pl.BlockSpec((d, tn), lambda i, j: (0, c0 + j))],
        out_specs=pl.BlockSpec((tm, tn), lambda i, j: (i, j)),
        out_shape=jax.ShapeDtypeStruct((n, ncols), out_dtype),
        scratch_shapes=[pltpu.VMEM((tm, d), BF16)],
        compiler_params=_cparams("arbitrary", "arbitrary"),
        name="inproj",
    )(x, g, w)


def _bias_tile_kernel(rb_ref, o_ref):
    h = pl.program_id(0)
    dlt = pl.program_id(1)
    c = lax.broadcasted_iota(jnp.int32, (MOBA_BLOCK, MOBA_BLOCK), 0)
    r = lax.broadcasted_iota(jnp.int32, (MOBA_BLOCK, MOBA_BLOCK), 1)
    d = dlt * MOBA_BLOCK + r - c
    n = jnp.maximum(d, 0)
    max_exact = N_BUCKETS // 2
    nf = jnp.maximum(n, 1).astype(F32)
    large = max_exact + (jnp.log(nf / max_exact) / math.log(REL_MAX_DIST / max_exact)
                         * (N_BUCKETS - max_exact)).astype(jnp.int32)
    large = jnp.minimum(large, N_BUCKETS - 1)
    bucket = jnp.where(n < max_exact, n, large)
    val = jnp.zeros((MOBA_BLOCK, MOBA_BLOCK), F32)
    for b in range(N_BUCKETS):
        val = jnp.where(bucket == b, rb_ref[b, h], val)
    o_ref[0, 0] = jnp.where(d >= 0, val * LOG2E, NEG_INF)


def _bias_tiles(rel_bias, n_blk):
    return pl.pallas_call(
        _bias_tile_kernel,
        grid=(N_HEADS, n_blk),
        in_specs=[pl.BlockSpec(memory_space=pltpu.SMEM)],
        out_specs=pl.BlockSpec((1, 1, MOBA_BLOCK, MOBA_BLOCK), lambda h, d: (h, d, 0, 0)),
        out_shape=jax.ShapeDtypeStruct((N_HEADS, n_blk, MOBA_BLOCK, MOBA_BLOCK), F32),
        compiler_params=_cparams("arbitrary", "arbitrary"),
        name="bias_tiles",
    )(rel_bias)


def _qkprep_kernel(q_ref, k_ref, v_ref, gq_ref, gk_ref, qat_ref, ka_ref, vt_ref, kmean_ref):
    i = pl.program_id(1)

    @pl.when(i == 0)
    def _():
        kmean_ref[...] = jnp.zeros_like(kmean_ref)

    col = lax.broadcasted_iota(jnp.int32, (MOBA_BLOCK, LANES), 1)
    onehot = jnp.where(col == i, 1.0, 0.0).astype(BF16)
    slot = lax.broadcasted_iota(jnp.int32, (LANES, MOBA_BLOCK), 0)
    past = slot < i
    scale = HEAD_DIM ** -0.5
    for h in range(N_HEADS):
        sl = slice(h * HEAD_DIM, (h + 1) * HEAD_DIM)
        lo = 2 * h * HEAD_DIM
        qnt = _rms(q_ref[:, sl].astype(F32), gq_ref[...]).T
        kn = _rms(k_ref[:, sl].astype(F32), gk_ref[...])
        gate = jnp.dot(kmean_ref[h], qnt, precision=lax.Precision.HIGHEST,
                       preferred_element_type=F32)
        g = jnp.where(past, gate, NEG_INF)
        sel = slot < 0
        for _ in range(MOBA_TOPK):
            m = jnp.max(g, axis=0, keepdims=True)
            idx = jnp.min(jnp.where(g == m, slot, LANES), axis=0, keepdims=True)
            hit = slot == idx
            sel = jnp.logical_or(sel, hit)
            g = jnp.where(hit, BELOW_NEG_INF, g)
        keep = jnp.logical_or(jnp.logical_and(sel, past), slot == i)
        qat_ref[lo:lo + HEAD_DIM, :] = (qnt * (scale * LOG2E)).astype(BF16)
        qat_ref[lo + HEAD_DIM:lo + 2 * HEAD_DIM, :] = jnp.where(keep, 0.0, NEG_INF).astype(BF16)
        ka_ref[:, lo:lo + HEAD_DIM] = kn.astype(BF16)
        ka_ref[:, lo + HEAD_DIM:lo + 2 * HEAD_DIM] = onehot
        kmean_ref[h, pl.ds(i, 1), :] = jnp.mean(kn, axis=0, keepdims=True)
        vt_ref[h * V_ROWS:h * V_ROWS + HEAD_DIM, :] = v_ref[:, sl].astype(F32).T.astype(BF16)
        vt_ref[h * V_ROWS + HEAD_DIM:(h + 1) * V_ROWS, :] = jnp.ones((V_ROWS - HEAD_DIM, MOBA_BLOCK), BF16)


def _qkprep(zf, gq, gk, bsz, seq):
    n = bsz * seq
    n_blk = seq // MOBA_BLOCK
    assert n_blk <= LANES
    cq = OFF_D // W_D
    row = lambda b, i: b * n_blk + i
    return pl.pallas_call(
        _qkprep_kernel,
        grid=(bsz, n_blk),
        in_specs=[pl.BlockSpec((MOBA_BLOCK, W_D), lambda b, i: (row(b, i), cq)),
                  pl.BlockSpec((MOBA_BLOCK, W_D), lambda b, i: (row(b, i), cq + 1)),
                  pl.BlockSpec((MOBA_BLOCK, W_D), lambda b, i: (row(b, i), cq + 2)),
                  pl.BlockSpec((1, HEAD_DIM), lambda b, i: (0, 0)),
                  pl.BlockSpec((1, HEAD_DIM), lambda b, i: (0, 0))],
        out_specs=[pl.BlockSpec((2 * W_D, MOBA_BLOCK), lambda b, i: (0, row(b, i))),
                   pl.BlockSpec((MOBA_BLOCK, 2 * W_D), lambda b, i: (row(b, i), 0)),
                   pl.BlockSpec((N_HEADS * V_ROWS, MOBA_BLOCK), lambda b, i: (0, row(b, i)))],
        out_shape=[jax.ShapeDtypeStruct((2 * W_D, n), BF16),
                   jax.ShapeDtypeStruct((n, 2 * W_D), BF16),
                   jax.ShapeDtypeStruct((N_HEADS * V_ROWS, n), BF16)],
        scratch_shapes=[pltpu.VMEM((N_HEADS, LANES, HEAD_DIM), F32)],
        compiler_params=_cparams("arbitrary", "arbitrary"),
        name="qkprep",
    )(zf, zf, zf, gq, gk)


def _attn_kernel(qat_ref, ka_ref, vt_ref, bias_ref, o_ref, s_ref, m_ref, acc_ref):
    t = pl.program_id(2)
    n_groups = (Q_BLOCKS * t + Q_BLOCKS - 1) // KV_BLOCKS + 1
    kv = KV_BLOCKS * MOBA_BLOCK
    m_ref[...] = jnp.full(m_ref.shape, NEG_INF, F32)
    acc_ref[...] = jnp.zeros(acc_ref.shape, F32)

    def scores(g, carry):
        r0 = pl.multiple_of(g * kv, kv)
        s = _dot(ka_ref[pl.ds(r0, kv), :], qat_ref[...])
        for a in range(Q_BLOCKS):
            cols = slice(a * MOBA_BLOCK, (a + 1) * MOBA_BLOCK)
            mx = m_ref[:, cols]
            for jj in range(KV_BLOCKS):
                dlt = jnp.maximum(Q_BLOCKS * t + a - (g * KV_BLOCKS + jj), 0)
                part = s[jj * MOBA_BLOCK:(jj + 1) * MOBA_BLOCK, cols] + bias_ref[0, dlt]
                rj = pl.multiple_of(r0 + jj * MOBA_BLOCK, MOBA_BLOCK)
                s_ref[pl.ds(rj, MOBA_BLOCK), cols] = part
                mx = jnp.maximum(mx, jnp.max(part, axis=0, keepdims=True))
            m_ref[:, cols] = mx
        return carry

    lax.fori_loop(0, n_groups, scores, 0)

    def values(g, carry):
        r0 = pl.multiple_of(g * kv, kv)
        m = m_ref[...]
        pv = acc_ref[...]
        for jj in range(KV_BLOCKS):
            rj = pl.multiple_of(r0 + jj * MOBA_BLOCK, MOBA_BLOCK)
            p = jnp.exp2(s_ref[pl.ds(rj, MOBA_BLOCK), :] - m)
            pv = pv + _dot(vt_ref[:, pl.ds(rj, MOBA_BLOCK)], p.astype(BF16))
        acc_ref[...] = pv
        return carry

    lax.fori_loop(0, n_groups, values, 0)
    o_ref[...] = (acc_ref[:HEAD_DIM, :] / acc_ref[HEAD_DIM:HEAD_DIM + 1, :]).T.astype(o_ref.dtype)


def _attn(qat, ka, vt, bias_t, bsz, seq):
    n = bsz * seq
    n_blk = seq // MOBA_BLOCK
    assert n_blk % KV_BLOCKS == 0 and n_blk % Q_BLOCKS == 0
    tq = Q_BLOCKS * MOBA_BLOCK
    nt = n_blk // Q_BLOCKS
    return pl.pallas_call(
        _attn_kernel,
        grid=(N_HEADS, bsz, nt),
        in_specs=[pl.BlockSpec((2 * HEAD_DIM, tq), lambda h, b, t: (h, b * nt + t)),
                  pl.BlockSpec((seq, 2 * HEAD_DIM), lambda h, b, t: (b, h)),
                  pl.BlockSpec((V_ROWS, seq), lambda h, b, t: (h, b)),
                  pl.BlockSpec((1, n_blk, MOBA_BLOCK, MOBA_BLOCK), lambda h, b, t: (h, 0, 0, 0))],
        out_specs=pl.BlockSpec((tq, HEAD_DIM), lambda h, b, t: (b * nt + t, h)),
        out_shape=jax.ShapeDtypeStruct((n, W_D), BF16),
        scratch_shapes=[pltpu.VMEM((seq, tq), F32), pltpu.VMEM((1, tq), F32),
                        pltpu.VMEM((V_ROWS, tq), F32)],
        compiler_params=_cparams("arbitrary", "arbitrary", "arbitrary"),
        name="attn",
    )(qat, ka, vt, bias_t)


def _branch_kernel(ab_ref, ac_ref, ax_ref, ba_ref, bg_ref, cu_ref, cv_ref,
                   wa_ref, wb_ref, bb_ref, lbg_ref, lbb_ref, lcg_ref, lcb_ref, ws_ref, bs_ref,
                   pa_ref, pb_ref, pc_ref, exta_ref, extb_ref):
    tm = ab_ref.shape[0]
    f32 = lambda ref, *idx: ref[idx if idx else ...].astype(F32)

    @pl.when(pl.program_id(1) == 0)
    def _():
        exta_ref[0:HALO_A, :] = jnp.zeros((HALO_A, W_A), F32)
        extb_ref[0, 0:HALO_B, :] = jnp.zeros((HALO_B, W_B), F32)

    exta_ref[HALO_A:HALO_A + tm, :] = f32(ac_ref) * f32(ax_ref)
    extb_ref[0, HALO_B:HALO_B + tm, :] = f32(ba_ref) * jax.nn.sigmoid(f32(bg_ref))
    shifted_rows = tm + HALO_B - SUBLANES
    for s in range(1, SUBLANES):
        extb_ref[s, 0:shifted_rows, :] = extb_ref[0, s:s + shifted_rows, :]

    for r in range(0, tm, CONV_ROWS):
        ya = None
        for k in range(CONV_A):
            off = r + HALO_A - (CONV_A - 1) + k
            term = wa_ref[k:k + 1, :] * exta_ref[off:off + CONV_ROWS, :]
            ya = term if ya is None else ya + term
        pa_ref[r:r + CONV_ROWS, :] = (f32(ab_ref, slice(r, r + CONV_ROWS), slice(None)) * ya).astype(pa_ref.dtype)

        yb = None
        for k in range(CONV_B):
            a, s = divmod(HALO_B - (CONV_B - 1) + k, SUBLANES)
            off = r + a * SUBLANES
            term = wb_ref[k:k + 1, :] * extb_ref[s, off:off + CONV_ROWS, :]
            yb = term if yb is None else yb + term
        hb = _layer_norm(yb + bb_ref[...], lbg_ref[...], lbb_ref[...])
        pb_ref[r:r + CONV_ROWS, :] = jax.nn.silu(hb).astype(pb_ref.dtype)

    exta_ref[0:HALO_A, :] = exta_ref[tm:tm + HALO_A, :]
    extb_ref[0, 0:HALO_B, :] = extb_ref[0, tm:tm + HALO_B, :]

    t_idx = lax.broadcasted_iota(jnp.int32, (GMLP_CHUNK, GMLP_CHUNK), 0)
    s_idx = lax.broadcasted_iota(jnp.int32, (GMLP_CHUNK, GMLP_CHUNK), 1)
    wg = W_C // GMLP_GROUPS
    wm = [jnp.where(s_idx <= t_idx, ws_ref[g], 0.0).astype(BF16) for g in range(GMLP_GROUPS)]
    for r in range(0, tm, GMLP_CHUNK):
        rows = slice(r, r + GMLP_CHUNK)
        u = jax.nn.gelu(f32(cu_ref, rows, slice(None)))
        vn = _layer_norm(jax.nn.gelu(f32(cv_ref, rows, slice(None))), lcg_ref[...], lcb_ref[...]).astype(BF16)
        for g in range(GMLP_GROUPS):
            cols = slice(g * wg, (g + 1) * wg)
            sv = _dot(wm[g], vn[:, cols]) + bs_ref[:, cols]
            pc_ref[rows, cols] = (u[:, cols] * sv).astype(pc_ref.dtype)


def _branch(zf, lw, bsz, seq, tm):
    n = bsz * seq
    assert seq % tm == 0 and tm % GMLP_CHUNK == 0 and tm % CONV_ROWS == 0
    nt = seq // tm
    zspec = lambda c: pl.BlockSpec((tm, W_A), lambda b, t: (b * nt + t, c))
    full = lambda a: pl.BlockSpec(a.shape, lambda b, t: (0,) * a.ndim)
    params = [lw["conv_a_w"], lw["conv_b_w"], lw["conv_b_bias"], lw["ln_b_g"], lw["ln_b_b"],
              lw["ln_c_g"], lw["ln_c_b"], lw["w_spatial"], lw["b_spatial2"]]
    ospec = pl.BlockSpec((tm, W_A), lambda b, t: (b * nt + t, 0))
    return pl.pallas_call(
        _branch_kernel,
        grid=(bsz, nt),
        in_specs=[zspec(c) for c in range(7)] + [full(a) for a in params],
        out_specs=[ospec, ospec, ospec],
        out_shape=[jax.ShapeDtypeStruct((n, W_A), BF16)] * 3,
        scratch_shapes=[pltpu.VMEM((tm + HALO_A, W_A), F32),
                        pltpu.VMEM((SUBLANES, tm + HALO_B, W_B), F32)],
        compiler_params=_cparams("arbitrary", "arbitrary"),
        name="branch",
    )(*([zf] * 7), *params)


def _merge_kernel(pa_ref, pb_ref, pc_ref, pd_ref, wa_ref, wb_ref, wc_ref, wd_ref,
                  ga_ref, gb_ref, gc_ref, gd_ref, o_ref):
    acc = None
    for p_ref, w_ref, g_ref in ((pa_ref, wa_ref, ga_ref), (pb_ref, wb_ref, gb_ref),
                                (pc_ref, wc_ref, gc_ref), (pd_ref, wd_ref, gd_ref)):
        gate = 0.5 * jnp.tanh(0.5 * g_ref[...].astype(F32)) + 0.5
        y = gate * _dot(p_ref[...], w_ref[...])
        acc = y if acc is None else acc + y
    o_ref[...] = acc.astype(o_ref.dtype)


def _merge(ps, ws, z, d_model, tm, tn):
    n = z.shape[0]
    assert n % tm == 0 and d_model % tn == 0
    nj = d_model // tn
    pspec = pl.BlockSpec((tm, W_A), lambda i, j: (i, 0))
    wspec = pl.BlockSpec((W_A, tn), lambda i, j: (0, j))
    assert OFF_G % tn == 0
    gspec = lambda k: pl.BlockSpec((tm, tn), lambda i, j: (i, OFF_G // tn + k * nj + j))
    return pl.pallas_call(
        _merge_kernel,
        grid=(n // tm, nj),
        in_specs=[pspec] * 4 + [wspec] * 4 + [gspec(k) for k in range(N_BRANCH)],
        out_specs=pl.BlockSpec((tm, tn), lambda i, j: (i, j)),
        out_shape=jax.ShapeDtypeStruct((n, d_model), BF16),
        compiler_params=_cparams("arbitrary", "arbitrary"),
        name="merge",
    )(*ps, *ws, z, z, z, z)


def _oproj_kernel(x_ref, m_ref, w_ref, o_ref):
    o_ref[...] = x_ref[...] + _dot(m_ref[...], w_ref[...])


def _oproj(x, merged, w, tm, tn):
    n, d = x.shape
    assert n % tm == 0 and d % tn == 0
    return pl.pallas_call(
        _oproj_kernel,
        grid=(n // tm, d // tn),
        in_specs=[pl.BlockSpec((tm, tn), lambda i, j: (i, j)),
                  pl.BlockSpec((tm, d), lambda i, j: (i, 0)),
                  pl.BlockSpec((d, tn), lambda i, j: (0, j))],
        out_specs=pl.BlockSpec((tm, tn), lambda i, j: (i, j)),
        out_shape=jax.ShapeDtypeStruct((n, d), F32),
        compiler_params=_cparams("arbitrary", "arbitrary"),
        name="oproj",
    )(x, merged, w)


def _mlp_kernel(x_ref, g_ref, w1_ref, w2_ref, o_ref, h_ref):
    @pl.when(pl.program_id(1) == 0)
    def _():
        _rmsnorm_to(x_ref, g_ref, h_ref)
        o_ref[...] = x_ref[...]
    a = jnp.square(jnp.maximum(_dot(h_ref[...], w1_ref[...]), 0.0)).astype(BF16)
    o_ref[...] += _dot(a, w2_ref[...])


def _mlp(x, g, w1, w2, tm, tf):
    n, d = x.shape
    d_ff = w1.shape[1]
    assert n % tm == 0 and d_ff % tf == 0
    return pl.pallas_call(
        _mlp_kernel,
        grid=(n // tm, d_ff // tf),
        in_specs=[pl.BlockSpec((tm, d), lambda i, f: (i, 0)),
                  pl.BlockSpec((1, d), lambda i, f: (0, 0)),
                  pl.BlockSpec((d, tf), lambda i, f: (0, f)),
                  pl.BlockSpec((tf, d), lambda i, f: (f, 0))],
        out_specs=pl.BlockSpec((tm, d), lambda i, f: (i, 0)),
        out_shape=jax.ShapeDtypeStruct((n, d), F32),
        scratch_shapes=[pltpu.VMEM((tm, d), BF16)],
        compiler_params=_cparams("arbitrary", "arbitrary"),
        name="mlp",
    )(x, g, w1, w2)


def _tiles(n, seq):
    return dict(tm_mm=min(1024, n), tm_branch=min(512, seq))


def kernel(x, rel_bias, norm_mix_g, w_in, conv_a_w, w_out_a, conv_b_w, conv_b_bias, ln_b_g, ln_b_b,
           w_out_b, ln_c_g, ln_c_b, w_spatial, b_spatial, w_out_c, q_norm_g, k_norm_g, w_out_d, w_o,
           norm_mlp_g, w_mlp_in, w_mlp_out):
    bsz, seq, d_model = x.shape
    n = bsz * seq
    assert seq % MOBA_BLOCK == 0 and w_in.shape[-1] == OFF_G + N_BRANCH * d_model
    t = _tiles(n, seq)
    tm = t["tm_mm"]
    row = lambda a: a[:, None, :]

    layers = dict(
        norm_mix_g=row(norm_mix_g), w_in=w_in.astype(BF16),
        conv_a_w=conv_a_w, conv_b_w=conv_b_w, conv_b_bias=row(conv_b_bias),
        ln_b_g=row(ln_b_g), ln_b_b=row(ln_b_b), ln_c_g=row(ln_c_g), ln_c_b=row(ln_c_b),
        w_spatial=w_spatial,
        b_spatial2=jnp.repeat(jnp.swapaxes(b_spatial, 1, 2), W_C // GMLP_GROUPS, axis=2),
        q_norm_g=row(q_norm_g), k_norm_g=row(k_norm_g),
        w_out_a=w_out_a.astype(BF16), w_out_b=w_out_b.astype(BF16),
        w_out_c=w_out_c.astype(BF16), w_out_d=w_out_d.astype(BF16),
        w_o=w_o.astype(BF16), norm_mlp_g=row(norm_mlp_g),
        w_mlp_in=w_mlp_in.astype(BF16), w_mlp_out=w_mlp_out.astype(BF16),
    )
    bias_t = _bias_tiles(rel_bias, seq // MOBA_BLOCK)

    def layer(xc, lw):
        z = _inproj(xc, lw["norm_mix_g"], lw["w_in"], 0, lw["w_in"].shape[-1], BF16, tm, 1024)
        qat, ka, vt = _qkprep(z, lw["q_norm_g"], lw["k_norm_g"], bsz, seq)
        pd = _attn(qat, ka, vt, bias_t, bsz, seq)
        pa, pb, pc = _branch(z, lw, bsz, seq, t["tm_branch"])
        merged = _merge((pa, pb, pc, pd),
                        (lw["w_out_a"], lw["w_out_b"], lw["w_out_c"], lw["w_out_d"]),
                        z, d_model, tm, 512)
        xc = _oproj(xc, merged, lw["w_o"], tm, 1024)
        xc = _mlp(xc, lw["norm_mlp_g"], lw["w_mlp_in"], lw["w_mlp_out"], tm, 512)
        return xc, None

    out, _ = lax.scan(layer, x.reshape(n, d_model), layers)
    return out.reshape(bsz, seq, d_model)
```

```python
import functools
import math

import jax
import jax.numpy as jnp
from jax import lax
from jax.experimental import pallas as pl
from jax.experimental.pallas import tpu as pltpu

F32 = jnp.float32
BF16 = jnp.bfloat16

W_A = 512
CONV_A = 3
W_B = 512
CONV_B = 31
W_C = 512
GMLP_CHUNK = 128
GMLP_GROUPS = 4
N_HEADS = 4
HEAD_DIM = 128
W_D = N_HEADS * HEAD_DIM
MOBA_BLOCK = 256
MOBA_TOPK = 3
N_BUCKETS = 32
REL_MAX_DIST = 2048
N_BRANCH = 4
EPS = 1e-6
NEG_INF = -1e30
BELOW_NEG_INF = -3e38

OFF_A = 0
OFF_B = OFF_A + 3 * W_A
OFF_C = OFF_B + 2 * W_B
OFF_D = OFF_C + 2 * W_C
OFF_G = OFF_D + 3 * W_D

LANES = 128
SUBLANES = 8
VMEM_LIMIT_BYTES = 56 * 1024 * 1024
HALO_A = 8
HALO_B = 32
CONV_ROWS = 32
KV_BLOCKS = 8
Q_BLOCKS = 4
V_ROWS = HEAD_DIM + 16
LOG2E = math.log2(math.e)


def _cparams(*semantics):
    return pltpu.CompilerParams(dimension_semantics=semantics,
                                vmem_limit_bytes=VMEM_LIMIT_BYTES)


def _dot(a, b):
    return jnp.dot(a, b, preferred_element_type=F32)


def _dot_nt(a, b, precision=None):
    return lax.dot_general(a, b, (((1,), (1,)), ((), ())), precision=precision,
                           preferred_element_type=F32)


def _rms(x, g):
    return x * lax.rsqrt(jnp.mean(x * x, axis=-1, keepdims=True) + EPS) * g


def _layer_norm(x, g, b):
    mu = jnp.mean(x, axis=-1, keepdims=True)
    xc = x - mu
    return xc * lax.rsqrt(jnp.mean(xc * xc, axis=-1, keepdims=True) + EPS) * g + b


def _rmsnorm_to(x_ref, g_ref, h_ref, rows=128):
    def body(c, carry):
        r = pl.multiple_of(c * rows, rows)
        h_ref[pl.ds(r, rows), :] = _rms(x_ref[pl.ds(r, rows), :], g_ref[...]).astype(h_ref.dtype)
        return carry
    lax.fori_loop(0, x_ref.shape[0] // rows, body, 0)


def _inproj_kernel(x_ref, g_ref, w_ref, o_ref, h_ref):
    @pl.when(pl.program_id(1) == 0)
    def _():
        _rmsnorm_to(x_ref, g_ref, h_ref)
    o_ref[...] = _dot(h_ref[...], w_ref[...]).astype(o_ref.dtype)


def _inproj(x, g, w, col_lo, ncols, out_dtype, tm, tn):
    n, d = x.shape
    assert n % tm == 0 and ncols % tn == 0 and col_lo % tn == 0
    c0 = col_lo // tn
    return pl.pallas_call(
        _inproj_kernel,
        grid=(n // tm, ncols // tn),
        in_specs=[pl.BlockSpec((tm, d), lambda i, j: (i, 0)),
                  pl.BlockSpec((1, d), lambda i, j: (0, 0)),
                  pl.BlockSpec((d, tn), lambda i, j: (0, c0 + j))],
        out_specs=pl.BlockSpec((tm, tn), lambda i, j: (i, j)),
        out_shape=jax.ShapeDtypeStruct((n, ncols), out_dtype),
        scratch_shapes=[pltpu.VMEM((tm, d), BF16)],
        compiler_params=_cparams("arbitrary", "arbitrary"),
        name="inproj",
    )(x, g, w)


def _bias_tile_kernel(rb_ref, o_ref):
    h = pl.program_id(0)
    dlt = pl.program_id(1)
    c = lax.broadcasted_iota(jnp.int32, (MOBA_BLOCK, MOBA_BLOCK), 0)
    r = lax.broadcasted_iota(jnp.int32, (MOBA_BLOCK, MOBA_BLOCK), 1)
    d = dlt * MOBA_BLOCK + r - c
    n = jnp.maximum(d, 0)
    max_exact = N_BUCKETS // 2
    nf = jnp.maximum(n, 1).astype(F32)
    large = max_exact + (jnp.log(nf / max_exact) / math.log(REL_MAX_DIST / max_exact)
                         * (N_BUCKETS - max_exact)).astype(jnp.int32)
    large = jnp.minimum(large, N_BUCKETS - 1)
    bucket = jnp.where(n < max_exact, n, large)
    val = jnp.zeros((MOBA_BLOCK, MOBA_BLOCK), F32)
    for b in range(N_BUCKETS):
        val = jnp.where(bucket == b, rb_ref[b, h], val)
    o_ref[0, 0] = jnp.where(d >= 0, val * LOG2E, NEG_INF)


def _bias_tiles(rel_bias, n_blk):
    return pl.pallas_call(
        _bias_tile_kernel,
        grid=(N_HEADS, n_blk),
        in_specs=[pl.BlockSpec(memory_space=pltpu.SMEM)],
        out_specs=pl.BlockSpec((1, 1, MOBA_BLOCK, MOBA_BLOCK), lambda h, d: (h, d, 0, 0)),
        out_shape=jax.ShapeDtypeStruct((N_HEADS, n_blk, MOBA_BLOCK, MOBA_BLOCK), F32),
        compiler_params=_cparams("arbitrary", "arbitrary"),
        name="bias_tiles",
    )(rel_bias)


def _qkprep_kernel(q_ref, k_ref, v_ref, gq_ref, gk_ref, qat_ref, ka_ref, vt_ref, kmean_ref):
    i = pl.program_id(1)

    @pl.when(i == 0)
    def _():
        kmean_ref[...] = jnp.zeros_like(kmean_ref)

    col = lax.broadcasted_iota(jnp.int32, (MOBA_BLOCK, LANES), 1)
    onehot = jnp.where(col == i, 1.0, 0.0).astype(BF16)
    slot = lax.broadcasted_iota(jnp.int32, (LANES, MOBA_BLOCK), 0)
    past = slot < i
    scale = HEAD_DIM ** -0.5
    for h in range(N_HEADS):
        sl = slice(h * HEAD_DIM, (h + 1) * HEAD_DIM)
        lo = 2 * h * HEAD_DIM
        qnt = _rms(q_ref[:, sl].astype(F32), gq_ref[...]).T
        kn = _rms(k_ref[:, sl].astype(F32), gk_ref[...])
        gate = jnp.dot(kmean_ref[h], qnt, precision=lax.Precision.HIGHEST,
                       preferred_element_type=F32)
        g = jnp.where(past, gate, NEG_INF)
        sel = slot < 0
        for _ in range(MOBA_TOPK):
            m = jnp.max(g, axis=0, keepdims=True)
            idx = jnp.min(jnp.where(g == m, slot, LANES), axis=0, keepdims=True)
            hit = slot == idx
            sel = jnp.logical_or(sel, hit)
            g = jnp.where(hit, BELOW_NEG_INF, g)
        keep = jnp.logical_or(jnp.logical_and(sel, past), slot == i)
        qat_ref[lo:lo + HEAD_DIM, :] = (qnt * (scale * LOG2E)).astype(BF16)
        qat_ref[lo + HEAD_DIM:lo + 2 * HEAD_DIM, :] = jnp.where(keep, 0.0, NEG_INF).astype(BF16)
        ka_ref[:, lo:lo + HEAD_DIM] = kn.astype(BF16)
        ka_ref[:, lo + HEAD_DIM:lo + 2 * HEAD_DIM] = onehot
        kmean_ref[h, pl.ds(i, 1), :] = jnp.mean(kn, axis=0, keepdims=True)
        vt_ref[h * V_ROWS:h * V_ROWS + HEAD_DIM, :] = v_ref[:, sl].astype(F32).T.astype(BF16)
        vt_ref[h * V_ROWS + HEAD_DIM:(h + 1) * V_ROWS, :] = jnp.ones((V_ROWS - HEAD_DIM, MOBA_BLOCK), BF16)


def _qkprep(zf, gq, gk, bsz, seq):
    n = bsz * seq
    n_blk = seq // MOBA_BLOCK
    assert n_blk <= LANES
    cq = OFF_D // W_D
    row = lambda b, i: b * n_blk + i
    return pl.pallas_call(
        _qkprep_kernel,
        grid=(bsz, n_blk),
        in_specs=[pl.BlockSpec((MOBA_BLOCK, W_D), lambda b, i: (row(b, i), cq)),
                  pl.BlockSpec((MOBA_BLOCK, W_D), lambda b, i: (row(b, i), cq + 1)),
                  pl.BlockSpec((MOBA_BLOCK, W_D), lambda b, i: (row(b, i), cq + 2)),
                  pl.BlockSpec((1, HEAD_DIM), lambda b, i: (0, 0)),
                  pl.BlockSpec((1, HEAD_DIM), lambda b, i: (0, 0))],
        out_specs=[pl.BlockSpec((2 * W_D, MOBA_BLOCK), lambda b, i: (0, row(b, i))),
                   pl.BlockSpec((MOBA_BLOCK, 2 * W_D), lambda b, i: (row(b, i), 0)),
                   pl.BlockSpec((N_HEADS * V_ROWS, MOBA_BLOCK), lambda b, i: (0, row(b, i)))],
        out_shape=[jax.ShapeDtypeStruct((2 * W_D, n), BF16),
                   jax.ShapeDtypeStruct((n, 2 * W_D), BF16),
                   jax.ShapeDtypeStruct((N_HEADS * V_ROWS, n), BF16)],
        scratch_shapes=[pltpu.VMEM((N_HEADS, LANES, HEAD_DIM), F32)],
        compiler_params=_cparams("arbitrary", "arbitrary"),
        name="qkprep",
    )(zf, zf, zf, gq, gk)


def _attn_kernel(qat_ref, ka_ref, vt_ref, bias_ref, o_ref, s_ref):
    t = pl.program_id(2)
    n_groups = (Q_BLOCKS * t) // KV_BLOCKS + 1
    kv = KV_BLOCKS * MOBA_BLOCK

    def scores(n, c, buf):
        mx = None
        for g in range(n):
            s = _dot(ka_ref[g * kv:(g + 1) * kv, :],
                     qat_ref[:, c * MOBA_BLOCK:(c + 1) * MOBA_BLOCK])
            for jj in range(KV_BLOCKS):
                blk = g * KV_BLOCKS + jj
                dlt = jnp.maximum(Q_BLOCKS * t + c - blk, 0)
                part = s[jj * MOBA_BLOCK:(jj + 1) * MOBA_BLOCK, :] + bias_ref[0, dlt]
                s_ref[buf, blk * MOBA_BLOCK:(blk + 1) * MOBA_BLOCK, :] = part
                pm = jnp.max(part, axis=0, keepdims=True)
                mx = pm if mx is None else jnp.maximum(mx, pm)
        return mx

    def values(n, c, buf, m):
        pv = None
        for blk in range(n * KV_BLOCKS):
            rows = slice(blk * MOBA_BLOCK, (blk + 1) * MOBA_BLOCK)
            p = jnp.exp2(s_ref[buf, rows, :] - m)
            d = _dot(vt_ref[:, rows], p.astype(BF16))
            pv = d if pv is None else pv + d
        o_ref[c * MOBA_BLOCK:(c + 1) * MOBA_BLOCK, :] = (
            pv[:HEAD_DIM, :] / pv[HEAD_DIM:HEAD_DIM + 1, :]).T.astype(o_ref.dtype)

    def run(n):
        m = scores(n, 0, 0)
        for c in range(Q_BLOCKS):
            m_next = scores(n, c + 1, (c + 1) % 2) if c + 1 < Q_BLOCKS else None
            values(n, c, c % 2, m)
            m = m_next

    for n in range(1, ka_ref.shape[0] // kv + 1):
        pl.when(n_groups == n)(functools.partial(run, n))


def _attn(qat, ka, vt, bias_t, bsz, seq):
    n = bsz * seq
    n_blk = seq // MOBA_BLOCK
    assert n_blk % KV_BLOCKS == 0 and KV_BLOCKS % Q_BLOCKS == 0
    tq = Q_BLOCKS * MOBA_BLOCK
    nt = n_blk // Q_BLOCKS
    return pl.pallas_call(
        _attn_kernel,
        grid=(N_HEADS, bsz, nt),
        in_specs=[pl.BlockSpec((2 * HEAD_DIM, tq), lambda h, b, t: (h, b * nt + t)),
                  pl.BlockSpec((seq, 2 * HEAD_DIM), lambda h, b, t: (b, h)),
                  pl.BlockSpec((V_ROWS, seq), lambda h, b, t: (h, b)),
                  pl.BlockSpec((1, n_blk, MOBA_BLOCK, MOBA_BLOCK), lambda h, b, t: (h, 0, 0, 0))],
        out_specs=pl.BlockSpec((tq, HEAD_DIM), lambda h, b, t: (b * nt + t, h)),
        out_shape=jax.ShapeDtypeStruct((n, W_D), BF16),
        scratch_shapes=[pltpu.VMEM((2, seq, MOBA_BLOCK), F32)],
        compiler_params=_cparams("arbitrary", "arbitrary", "arbitrary"),
        name="attn",
    )(qat, ka, vt, bias_t)


def _branch_kernel(ab_ref, ac_ref, ax_ref, ba_ref, bg_ref, cu_ref, cv_ref,
                   wa_ref, wb_ref, bb_ref, lbg_ref, lbb_ref, lcg_ref, lcb_ref, ws_ref, bs_ref,
                   pa_ref, pb_ref, pc_ref, exta_ref, extb_ref):
    tm = ab_ref.shape[0]
    f32 = lambda ref, *idx: ref[idx if idx else ...].astype(F32)

    @pl.when(pl.program_id(1) == 0)
    def _():
        exta_ref[0:HALO_A, :] = jnp.zeros((HALO_A, W_A), F32)
        extb_ref[0, 0:HALO_B, :] = jnp.zeros((HALO_B, W_B), F32)

    exta_ref[HALO_A:HALO_A + tm, :] = f32(ac_ref) * f32(ax_ref)
    extb_ref[0, HALO_B:HALO_B + tm, :] = f32(ba_ref) * jax.nn.sigmoid(f32(bg_ref))
    shifted_rows = tm + HALO_B - SUBLANES
    for s in range(1, SUBLANES):
        extb_ref[s, 0:shifted_rows, :] = extb_ref[0, s:s + shifted_rows, :]

    for r in range(0, tm, CONV_ROWS):
        ya = None
        for k in range(CONV_A):
            off = r + HALO_A - (CONV_A - 1) + k
            term = wa_ref[k:k + 1, :] * exta_ref[off:off + CONV_ROWS, :]
            ya = term if ya is None else ya + term
        pa_ref[r:r + CONV_ROWS, :] = (f32(ab_ref, slice(r, r + CONV_ROWS), slice(None)) * ya).astype(pa_ref.dtype)

        yb = None
        for k in range(CONV_B):
            a, s = divmod(HALO_B - (CONV_B - 1) + k, SUBLANES)
            off = r + a * SUBLANES
            term = wb_ref[k:k + 1, :] * extb_ref[s, off:off + CONV_ROWS, :]
            yb = term if yb is None else yb + term
        hb = _layer_norm(yb + bb_ref[...], lbg_ref[...], lbb_ref[...])
        pb_ref[r:r + CONV_ROWS, :] = jax.nn.silu(hb).astype(pb_ref.dtype)

    exta_ref[0:HALO_A, :] = exta_ref[tm:tm + HALO_A, :]
    extb_ref[0, 0:HALO_B, :] = extb_ref[0, tm:tm + HALO_B, :]

    t_idx = lax.broadcasted_iota(jnp.int32, (GMLP_CHUNK, GMLP_CHUNK), 0)
    s_idx = lax.broadcasted_iota(jnp.int32, (GMLP_CHUNK, GMLP_CHUNK), 1)
    wg = W_C // GMLP_GROUPS
    wm = [jnp.where(s_idx <= t_idx, ws_ref[g], 0.0).astype(BF16) for g in range(GMLP_GROUPS)]
    for r in range(0, tm, GMLP_CHUNK):
        rows = slice(r, r + GMLP_CHUNK)
        u = jax.nn.gelu(f32(cu_ref, rows, slice(None)))
        vn = _layer_norm(jax.nn.gelu(f32(cv_ref, rows, slice(None))), lcg_ref[...], lcb_ref[...]).astype(BF16)
        for g in range(GMLP_GROUPS):
            cols = slice(g * wg, (g + 1) * wg)
            sv = _dot(wm[g], vn[:, cols]) + bs_ref[:, cols]
            pc_ref[rows, cols] = (u[:, cols] * sv).astype(pc_ref.dtype)


def _branch(zf, lw, bsz, seq, tm):
    n = bsz * seq
    assert seq % tm == 0 and tm % GMLP_CHUNK == 0 and tm % CONV_ROWS == 0
    nt = seq // tm
    zspec = lambda c: pl.BlockSpec((tm, W_A), lambda b, t: (b * nt + t, c))
    full = lambda a: pl.BlockSpec(a.shape, lambda b, t: (0,) * a.ndim)
    params = [lw["conv_a_w"], lw["conv_b_w"], lw["conv_b_bias"], lw["ln_b_g"], lw["ln_b_b"],
              lw["ln_c_g"], lw["ln_c_b"], lw["w_spatial"], lw["b_spatial2"]]
    ospec = pl.BlockSpec((tm, W_A), lambda b, t: (b * nt + t, 0))
    return pl.pallas_call(
        _branch_kernel,
        grid=(bsz, nt),
        in_specs=[zspec(c) for c in range(7)] + [full(a) for a in params],
        out_specs=[ospec, ospec, ospec],
        out_shape=[jax.ShapeDtypeStruct((n, W_A), BF16)] * 3,
        scratch_shapes=[pltpu.VMEM((tm + HALO_A, W_A), F32),
                        pltpu.VMEM((SUBLANES, tm + HALO_B, W_B), F32)],
        compiler_params=_cparams("arbitrary", "arbitrary"),
        name="branch",
    )(*([zf] * 7), *params)


def _merge_kernel(pa_ref, pb_ref, pc_ref, pd_ref, wa_ref, wb_ref, wc_ref, wd_ref,
                  ga_ref, gb_ref, gc_ref, gd_ref, o_ref):
    acc = None
    for p_ref, w_ref, g_ref in ((pa_ref, wa_ref, ga_ref), (pb_ref, wb_ref, gb_ref),
                                (pc_ref, wc_ref, gc_ref), (pd_ref, wd_ref, gd_ref)):
        gate = 0.5 * jnp.tanh(0.5 * g_ref[...].astype(F32)) + 0.5
        y = gate * _dot(p_ref[...], w_ref[...])
        acc = y if acc is None else acc + y
    o_ref[...] = acc.astype(o_ref.dtype)


def _merge(ps, ws, z, d_model, tm, tn):
    n = z.shape[0]
    assert n % tm == 0 and d_model % tn == 0
    nj = d_model // tn
    pspec = pl.BlockSpec((tm, W_A), lambda i, j: (i, 0))
    wspec = pl.BlockSpec((W_A, tn), lambda i, j: (0, j))
    assert OFF_G % tn == 0
    gspec = lambda k: pl.BlockSpec((tm, tn), lambda i, j: (i, OFF_G // tn + k * nj + j))
    return pl.pallas_call(
        _merge_kernel,
        grid=(n // tm, nj),
        in_specs=[pspec] * 4 + [wspec] * 4 + [gspec(k) for k in range(N_BRANCH)],
        out_specs=pl.BlockSpec((tm, tn), lambda i, j: (i, j)),
        out_shape=jax.ShapeDtypeStruct((n, d_model), BF16),
        compiler_params=_cparams("arbitrary", "arbitrary"),
        name="merge",
    )(*ps, *ws, z, z, z, z)


def _oproj_kernel(x_ref, m_ref, w_ref, o_ref):
    o_ref[...] = x_ref[...] + _dot(m_ref[...], w_ref[...])


def _oproj(x, merged, w, tm, tn):
    n, d = x.shape
    assert n % tm == 0 and d % tn == 0
    return pl.pallas_call(
        _oproj_kernel,
        grid=(n // tm, d // tn),
        in_specs=[pl.BlockSpec((tm, tn), lambda i, j: (i, j)),
                  pl.BlockSpec((tm, d), lambda i, j: (i, 0)),
                  pl.BlockSpec((d, tn), lambda i, j: (0, j))],
        out_specs=pl.BlockSpec((tm, tn), lambda i, j: (i, j)),
        out_shape=jax.ShapeDtypeStruct((n, d), F32),
        compiler_params=_cparams("arbitrary", "arbitrary"),
        name="oproj",
    )(x, merged, w)


def _mlp_kernel(x_ref, g_ref, w1_ref, w2_ref, o_ref, h_ref):
    @pl.when(pl.program_id(1) == 0)
    def _():
        _rmsnorm_to(x_ref, g_ref, h_ref)
        o_ref[...] = x_ref[...]
    a = jnp.square(jnp.maximum(_dot(h_ref[...], w1_ref[...]), 0.0)).astype(BF16)
    o_ref[...] += _dot(a, w2_ref[...])


def _mlp(x, g, w1, w2, tm, tf):
    n, d = x.shape
    d_ff = w1.shape[1]
    assert n % tm == 0 and d_ff % tf == 0
    return pl.pallas_call(
        _mlp_kernel,
        grid=(n // tm, d_ff // tf),
        in_specs=[pl.BlockSpec((tm, d), lambda i, f: (i, 0)),
                  pl.BlockSpec((1, d), lambda i, f: (0, 0)),
                  pl.BlockSpec((d, tf), lambda i, f: (0, f)),
                  pl.BlockSpec((tf, d), lambda i, f: (f, 0))],
        out_specs=pl.BlockSpec((tm, d), lambda i, f: (i, 0)),
        out_shape=jax.ShapeDtypeStruct((n, d), F32),
        scratch_shapes=[pltpu.VMEM((tm, d), BF16)],
        compiler_params=_cparams("arbitrary", "arbitrary"),
        name="mlp",
    )(x, g, w1, w2)


def _tiles(n, seq):
    return dict(tm_mm=min(1024, n), tm_branch=min(512, seq))


def kernel(x, rel_bias, norm_mix_g, w_in, conv_a_w, w_out_a, conv_b_w, conv_b_bias, ln_b_g, ln_b_b,
           w_out_b, ln_c_g, ln_c_b, w_spatial, b_spatial, w_out_c, q_norm_g, k_norm_g, w_out_d, w_o,
           norm_mlp_g, w_mlp_in, w_mlp_out):
    bsz, seq, d_model = x.shape
    n = bsz * seq
    assert seq % MOBA_BLOCK == 0 and w_in.shape[-1] == OFF_G + N_BRANCH * d_model
    t = _tiles(n, seq)
    tm = t["tm_mm"]
    row = lambda a: a[:, None, :]

    layers = dict(
        norm_mix_g=row(norm_mix_g), w_in=w_in,
        conv_a_w=conv_a_w, conv_b_w=conv_b_w, conv_b_bias=row(conv_b_bias),
        ln_b_g=row(ln_b_g), ln_b_b=row(ln_b_b), ln_c_g=row(ln_c_g), ln_c_b=row(ln_c_b),
        w_spatial=w_spatial,
        b_spatial2=jnp.repeat(jnp.swapaxes(b_spatial, 1, 2), W_C // GMLP_GROUPS, axis=2),
        q_norm_g=row(q_norm_g), k_norm_g=row(k_norm_g),
        w_out_a=w_out_a, w_out_b=w_out_b, w_out_c=w_out_c, w_out_d=w_out_d,
        w_o=w_o, norm_mlp_g=row(norm_mlp_g), w_mlp_in=w_mlp_in, w_mlp_out=w_mlp_out,
    )
    bias_t = _bias_tiles(rel_bias, seq // MOBA_BLOCK)

    def layer(xc, lw):
        bf = lambda name: lw[name].astype(BF16)
        z = _inproj(xc, lw["norm_mix_g"], bf("w_in"), 0, lw["w_in"].shape[-1], BF16, tm, 1024)
        qat, ka, vt = _qkprep(z, lw["q_norm_g"], lw["k_norm_g"], bsz, seq)
        pd = _attn(qat, ka, vt, bias_t, bsz, seq)
        pa, pb, pc = _branch(z, lw, bsz, seq, t["tm_branch"])
        merged = _merge((pa, pb, pc, pd),
                        (bf("w_out_a"), bf("w_out_b"), bf("w_out_c"), bf("w_out_d")),
                        z, d_model, tm, 512)
        xc = _oproj(xc, merged, bf("w_o"), tm, 1024)
        xc = _mlp(xc, lw["norm_mlp_g"], bf("w_mlp_in"), bf("w_mlp_out"), tm, 512)
        return xc, None

    out, _ = lax.scan(layer, x.reshape(n, d_model), layers)
    return out.reshape(bsz, seq, d_model)
```

```python
import functools
import math

import jax
import jax.numpy as jnp
from jax import lax
from jax.experimental import pallas as pl
from jax.experimental.pallas import tpu as pltpu

F32 = jnp.float32
BF16 = jnp.bfloat16

W_A = 512
CONV_A = 3
W_B = 512
CONV_B = 31
W_C = 512
GMLP_CHUNK = 128
GMLP_GROUPS = 4
N_HEADS = 4
HEAD_DIM = 128
W_D = N_HEADS * HEAD_DIM
MOBA_BLOCK = 256
MOBA_TOPK = 3
N_BUCKETS = 32
REL_MAX_DIST = 2048
N_BRANCH = 4
EPS = 1e-6
NEG_INF = -1e30
BELOW_NEG_INF = -3e38

OFF_A = 0
OFF_B = OFF_A + 3 * W_A
OFF_C = OFF_B + 2 * W_B
OFF_D = OFF_C + 2 * W_C
OFF_G = OFF_D + 3 * W_D

LANES = 128
SUBLANES = 8
VMEM_LIMIT_BYTES = 56 * 1024 * 1024
HALO_A = 8
HALO_B = 32
CONV_ROWS = 32
KV_BLOCKS = 8
Q_BLOCKS = 4
V_ROWS = HEAD_DIM + 16
LOG2E = math.log2(math.e)


def _cparams(*semantics):
    return pltpu.CompilerParams(dimension_semantics=semantics,
                                vmem_limit_bytes=VMEM_LIMIT_BYTES)


def _dot(a, b):
    return jnp.dot(a, b, preferred_element_type=F32)


def _dot_nt(a, b, precision=None):
    return lax.dot_general(a, b, (((1,), (1,)), ((), ())), precision=precision,
                           preferred_element_type=F32)


def _rms(x, g):
    return x * lax.rsqrt(jnp.mean(x * x, axis=-1, keepdims=True) + EPS) * g


def _layer_norm(x, g, b):
    mu = jnp.mean(x, axis=-1, keepdims=True)
    xc = x - mu
    return xc * lax.rsqrt(jnp.mean(xc * xc, axis=-1, keepdims=True) + EPS) * g + b


def _rmsnorm_to(x_ref, g_ref, h_ref, rows=128):
    def body(c, carry):
        r = pl.multiple_of(c * rows, rows)
        h_ref[pl.ds(r, rows), :] = _rms(x_ref[pl.ds(r, rows), :], g_ref[...]).astype(h_ref.dtype)
        return carry
    lax.fori_loop(0, x_ref.shape[0] // rows, body, 0)


def _inproj_kernel(l_ref, x_ref, g_ref, w_ref, o_ref, h_ref):
    @pl.when(pl.program_id(1) == 0)
    def _():
        _rmsnorm_to(x_ref, g_ref, h_ref)
    o_ref[...] = _dot(h_ref[...], w_ref[...]).astype(o_ref.dtype)


def _inproj(layer, x, g, w, col_lo, ncols, out_dtype, tm, tn):
    n, d = x.shape
    assert n % tm == 0 and ncols % tn == 0 and col_lo % tn == 0
    c0 = col_lo // tn
    return pl.pallas_call(
        _inproj_kernel,
        grid_spec=pltpu.PrefetchScalarGridSpec(
            num_scalar_prefetch=1,
            grid=(n // tm, ncols // tn),
            in_specs=[pl.BlockSpec((tm, d), lambda i, j, l: (i, 0)),
                      pl.BlockSpec((1, d), lambda i, j, l: (0, 0)),
                      pl.BlockSpec((None, d, tn), lambda i, j, l: (l[0], 0, c0 + j))],
            out_specs=pl.BlockSpec((tm, tn), lambda i, j, l: (i, j)),
            scratch_shapes=[pltpu.VMEM((tm, d), BF16)]),
        out_shape=jax.ShapeDtypeStruct((n, ncols), out_dtype),
        compiler_params=_cparams("arbitrary", "arbitrary"),
        name="inproj",
    )(layer, x, g, w)


def _bias_tile_kernel(rb_ref, o_ref):
    h = pl.program_id(0)
    dlt = pl.program_id(1)
    c = lax.broadcasted_iota(jnp.int32, (MOBA_BLOCK, MOBA_BLOCK), 0)
    r = lax.broadcasted_iota(jnp.int32, (MOBA_BLOCK, MOBA_BLOCK), 1)
    d = dlt * MOBA_BLOCK + r - c
    n = jnp.maximum(d, 0)
    max_exact = N_BUCKETS // 2
    nf = jnp.maximum(n, 1).astype(F32)
    large = max_exact + (jnp.log(nf / max_exact) / math.log(REL_MAX_DIST / max_exact)
                         * (N_BUCKETS - max_exact)).astype(jnp.int32)
    large = jnp.minimum(large, N_BUCKETS - 1)
    bucket = jnp.where(n < max_exact, n, large)
    val = jnp.zeros((MOBA_BLOCK, MOBA_BLOCK), F32)
    for b in range(N_BUCKETS):
        val = jnp.where(bucket == b, rb_ref[b, h], val)
    o_ref[0, 0] = jnp.where(d >= 0, val * LOG2E, NEG_INF)


def _bias_tiles(rel_bias, n_blk):
    return pl.pallas_call(
        _bias_tile_kernel,
        grid=(N_HEADS, n_blk),
        in_specs=[pl.BlockSpec(memory_space=pltpu.SMEM)],
        out_specs=pl.BlockSpec((1, 1, MOBA_BLOCK, MOBA_BLOCK), lambda h, d: (h, d, 0, 0)),
        out_shape=jax.ShapeDtypeStruct((N_HEADS, n_blk, MOBA_BLOCK, MOBA_BLOCK), F32),
        compiler_params=_cparams("arbitrary", "arbitrary"),
        name="bias_tiles",
    )(rel_bias)


def _qkprep_kernel(q_ref, k_ref, v_ref, gq_ref, gk_ref, qat_ref, ka_ref, vt_ref, kmean_ref):
    i = pl.program_id(1)

    @pl.when(i == 0)
    def _():
        kmean_ref[...] = jnp.zeros_like(kmean_ref)

    col = lax.broadcasted_iota(jnp.int32, (MOBA_BLOCK, LANES), 1)
    onehot = jnp.where(col == i, 1.0, 0.0).astype(BF16)
    slot = lax.broadcasted_iota(jnp.int32, (LANES, MOBA_BLOCK), 0)
    past = slot < i
    scale = HEAD_DIM ** -0.5
    for h in range(N_HEADS):
        sl = slice(h * HEAD_DIM, (h + 1) * HEAD_DIM)
        lo = 2 * h * HEAD_DIM
        qnt = _rms(q_ref[:, sl].astype(F32), gq_ref[...]).T
        kn = _rms(k_ref[:, sl].astype(F32), gk_ref[...])
        gate = jnp.dot(kmean_ref[h], qnt, precision=lax.Precision.HIGHEST,
                       preferred_element_type=F32)
        g = jnp.where(past, gate, NEG_INF)
        sel = slot < 0
        for _ in range(MOBA_TOPK):
            m = jnp.max(g, axis=0, keepdims=True)
            idx = jnp.min(jnp.where(g == m, slot, LANES), axis=0, keepdims=True)
            hit = slot == idx
            sel = jnp.logical_or(sel, hit)
            g = jnp.where(hit, BELOW_NEG_INF, g)
        keep = jnp.logical_or(jnp.logical_and(sel, past), slot == i)
        qat_ref[lo:lo + HEAD_DIM, :] = (qnt * (scale * LOG2E)).astype(BF16)
        qat_ref[lo + HEAD_DIM:lo + 2 * HEAD_DIM, :] = jnp.where(keep, 0.0, NEG_INF).astype(BF16)
        ka_ref[:, lo:lo + HEAD_DIM] = kn.astype(BF16)
        ka_ref[:, lo + HEAD_DIM:lo + 2 * HEAD_DIM] = onehot
        kmean_ref[h, pl.ds(i, 1), :] = jnp.mean(kn, axis=0, keepdims=True)
        vt_ref[h * V_ROWS:h * V_ROWS + HEAD_DIM, :] = v_ref[:, sl].astype(F32).T.astype(BF16)
        vt_ref[h * V_ROWS + HEAD_DIM:(h + 1) * V_ROWS, :] = jnp.ones((V_ROWS - HEAD_DIM, MOBA_BLOCK), BF16)


def _qkprep(zf, gq, gk, bsz, seq):
    n = bsz * seq
    n_blk = seq // MOBA_BLOCK
    assert n_blk <= LANES
    cq = OFF_D // W_D
    row = lambda b, i: b * n_blk + i
    return pl.pallas_call(
        _qkprep_kernel,
        grid=(bsz, n_blk),
        in_specs=[pl.BlockSpec((MOBA_BLOCK, W_D), lambda b, i: (row(b, i), cq)),
                  pl.BlockSpec((MOBA_BLOCK, W_D), lambda b, i: (row(b, i), cq + 1)),
                  pl.BlockSpec((MOBA_BLOCK, W_D), lambda b, i: (row(b, i), cq + 2)),
                  pl.BlockSpec((1, HEAD_DIM), lambda b, i: (0, 0)),
                  pl.BlockSpec((1, HEAD_DIM), lambda b, i: (0, 0))],
        out_specs=[pl.BlockSpec((2 * W_D, MOBA_BLOCK), lambda b, i: (0, row(b, i))),
                   pl.BlockSpec((MOBA_BLOCK, 2 * W_D), lambda b, i: (row(b, i), 0)),
                   pl.BlockSpec((N_HEADS * V_ROWS, MOBA_BLOCK), lambda b, i: (0, row(b, i)))],
        out_shape=[jax.ShapeDtypeStruct((2 * W_D, n), BF16),
                   jax.ShapeDtypeStruct((n, 2 * W_D), BF16),
                   jax.ShapeDtypeStruct((N_HEADS * V_ROWS, n), BF16)],
        scratch_shapes=[pltpu.VMEM((N_HEADS, LANES, HEAD_DIM), F32)],
        compiler_params=_cparams("arbitrary", "arbitrary"),
        name="qkprep",
    )(zf, zf, zf, gq, gk)


def _attn_kernel(qat_ref, ka_ref, vt_ref, bias_ref, o_ref, s_ref):
    t = pl.program_id(2)
    n_groups = (Q_BLOCKS * t) // KV_BLOCKS + 1
    kv = KV_BLOCKS * MOBA_BLOCK

    def scores(n, c, buf):
        mx = None
        for g in range(n):
            s = _dot(ka_ref[g * kv:(g + 1) * kv, :],
                     qat_ref[:, c * MOBA_BLOCK:(c + 1) * MOBA_BLOCK])
            for jj in range(KV_BLOCKS):
                blk = g * KV_BLOCKS + jj
                dlt = jnp.maximum(Q_BLOCKS * t + c - blk, 0)
                part = s[jj * MOBA_BLOCK:(jj + 1) * MOBA_BLOCK, :] + bias_ref[0, dlt]
                s_ref[buf, blk * MOBA_BLOCK:(blk + 1) * MOBA_BLOCK, :] = part
                pm = jnp.max(part, axis=0, keepdims=True)
                mx = pm if mx is None else jnp.maximum(mx, pm)
        return mx

    def values(n, c, buf, m):
        pv = None
        for blk in range(n * KV_BLOCKS):
            rows = slice(blk * MOBA_BLOCK, (blk + 1) * MOBA_BLOCK)
            p = jnp.exp2(s_ref[buf, rows, :] - m)
            d = _dot(vt_ref[:, rows], p.astype(BF16))
            pv = d if pv is None else pv + d
        o_ref[c * MOBA_BLOCK:(c + 1) * MOBA_BLOCK, :] = (
            pv[:HEAD_DIM, :] / pv[HEAD_DIM:HEAD_DIM + 1, :]).T.astype(o_ref.dtype)

    def run(n):
        m = scores(n, 0, 0)
        for c in range(Q_BLOCKS):
            m_next = scores(n, c + 1, (c + 1) % 2) if c + 1 < Q_BLOCKS else None
            values(n, c, c % 2, m)
            m = m_next

    for n in range(1, ka_ref.shape[0] // kv + 1):
        pl.when(n_groups == n)(functools.partial(run, n))


def _attn(qat, ka, vt, bias_t, bsz, seq):
    n = bsz * seq
    n_blk = seq // MOBA_BLOCK
    assert n_blk % KV_BLOCKS == 0 and KV_BLOCKS % Q_BLOCKS == 0
    tq = Q_BLOCKS * MOBA_BLOCK
    nt = n_blk // Q_BLOCKS
    return pl.pallas_call(
        _attn_kernel,
        grid=(N_HEADS, bsz, nt),
        in_specs=[pl.BlockSpec((2 * HEAD_DIM, tq), lambda h, b, t: (h, b * nt + t)),
                  pl.BlockSpec((seq, 2 * HEAD_DIM), lambda h, b, t: (b, h)),
                  pl.BlockSpec((V_ROWS, seq), lambda h, b, t: (h, b)),
                  pl.BlockSpec((1, n_blk, MOBA_BLOCK, MOBA_BLOCK), lambda h, b, t: (h, 0, 0, 0))],
        out_specs=pl.BlockSpec((tq, HEAD_DIM), lambda h, b, t: (b * nt + t, h)),
        out_shape=jax.ShapeDtypeStruct((n, W_D), BF16),
        scratch_shapes=[pltpu.VMEM((2, seq, MOBA_BLOCK), F32)],
        compiler_params=_cparams("arbitrary", "arbitrary", "arbitrary"),
        name="attn",
    )(qat, ka, vt, bias_t)


def _branch_kernel(ab_ref, ac_ref, ax_ref, ba_ref, bg_ref, cu_ref, cv_ref,
                   wa_ref, wb_ref, bb_ref, lbg_ref, lbb_ref, lcg_ref, lcb_ref, ws_ref, bs_ref,
                   pa_ref, pb_ref, pc_ref, exta_ref, extb_ref):
    tm = ab_ref.shape[0]
    f32 = lambda ref, *idx: ref[idx if idx else ...].astype(F32)

    @pl.when(pl.program_id(1) == 0)
    def _():
        exta_ref[0:HALO_A, :] = jnp.zeros((HALO_A, W_A), F32)
        extb_ref[0, 0:HALO_B, :] = jnp.zeros((HALO_B, W_B), F32)

    exta_ref[HALO_A:HALO_A + tm, :] = f32(ac_ref) * f32(ax_ref)
    extb_ref[0, HALO_B:HALO_B + tm, :] = f32(ba_ref) * jax.nn.sigmoid(f32(bg_ref))
    shifted_rows = tm + HALO_B - SUBLANES
    for s in range(1, SUBLANES):
        extb_ref[s, 0:shifted_rows, :] = extb_ref[0, s:s + shifted_rows, :]

    for r in range(0, tm, CONV_ROWS):
        ya = None
        for k in range(CONV_A):
            off = r + HALO_A - (CONV_A - 1) + k
            term = wa_ref[k:k + 1, :] * exta_ref[off:off + CONV_ROWS, :]
            ya = term if ya is None else ya + term
        pa_ref[r:r + CONV_ROWS, :] = (f32(ab_ref, slice(r, r + CONV_ROWS), slice(None)) * ya).astype(pa_ref.dtype)

        yb = None
        for k in range(CONV_B):
            a, s = divmod(HALO_B - (CONV_B - 1) + k, SUBLANES)
            off = r + a * SUBLANES
            term = wb_ref[k:k + 1, :] * extb_ref[s, off:off + CONV_ROWS, :]
            yb = term if yb is None else yb + term
        hb = _layer_norm(yb + bb_ref[...], lbg_ref[...], lbb_ref[...])
        pb_ref[r:r + CONV_ROWS, :] = jax.nn.silu(hb).astype(pb_ref.dtype)

    exta_ref[0:HALO_A, :] = exta_ref[tm:tm + HALO_A, :]
    extb_ref[0, 0:HALO_B, :] = extb_ref[0, tm:tm + HALO_B, :]

    t_idx = lax.broadcasted_iota(jnp.int32, (GMLP_CHUNK, GMLP_CHUNK), 0)
    s_idx = lax.broadcasted_iota(jnp.int32, (GMLP_CHUNK, GMLP_CHUNK), 1)
    wg = W_C // GMLP_GROUPS
    wm = [jnp.where(s_idx <= t_idx, ws_ref[g], 0.0).astype(BF16) for g in range(GMLP_GROUPS)]
    for r in range(0, tm, GMLP_CHUNK):
        rows = slice(r, r + GMLP_CHUNK)
        u = jax.nn.gelu(f32(cu_ref, rows, slice(None)))
        vn = _layer_norm(jax.nn.gelu(f32(cv_ref, rows, slice(None))), lcg_ref[...], lcb_ref[...]).astype(BF16)
        for g in range(GMLP_GROUPS):
            cols = slice(g * wg, (g + 1) * wg)
            sv = _dot(wm[g], vn[:, cols]) + bs_ref[:, cols]
            pc_ref[rows, cols] = (u[:, cols] * sv).astype(pc_ref.dtype)


def _branch(zf, lw, bsz, seq, tm):
    n = bsz * seq
    assert seq % tm == 0 and tm % GMLP_CHUNK == 0 and tm % CONV_ROWS == 0
    nt = seq // tm
    zspec = lambda c: pl.BlockSpec((tm, W_A), lambda b, t: (b * nt + t, c))
    full = lambda a: pl.BlockSpec(a.shape, lambda b, t: (0,) * a.ndim)
    params = [lw["conv_a_w"], lw["conv_b_w"], lw["conv_b_bias"], lw["ln_b_g"], lw["ln_b_b"],
              lw["ln_c_g"], lw["ln_c_b"], lw["w_spatial"], lw["b_spatial2"]]
    ospec = pl.BlockSpec((tm, W_A), lambda b, t: (b * nt + t, 0))
    return pl.pallas_call(
        _branch_kernel,
        grid=(bsz, nt),
        in_specs=[zspec(c) for c in range(7)] + [full(a) for a in params],
        out_specs=[ospec, ospec, ospec],
        out_shape=[jax.ShapeDtypeStruct((n, W_A), BF16)] * 3,
        scratch_shapes=[pltpu.VMEM((tm + HALO_A, W_A), F32),
                        pltpu.VMEM((SUBLANES, tm + HALO_B, W_B), F32)],
        compiler_params=_cparams("arbitrary", "arbitrary"),
        name="branch",
    )(*([zf] * 7), *params)


def _merge_kernel(l_ref, pa_ref, pb_ref, pc_ref, pd_ref, wa_ref, wb_ref, wc_ref, wd_ref,
                  ga_ref, gb_ref, gc_ref, gd_ref, o_ref):
    acc = None
    for p_ref, w_ref, g_ref in ((pa_ref, wa_ref, ga_ref), (pb_ref, wb_ref, gb_ref),
                                (pc_ref, wc_ref, gc_ref), (pd_ref, wd_ref, gd_ref)):
        gate = 0.5 * jnp.tanh(0.5 * g_ref[...].astype(F32)) + 0.5
        y = gate * _dot(p_ref[...], w_ref[...])
        acc = y if acc is None else acc + y
    o_ref[...] = acc.astype(o_ref.dtype)


def _merge(layer, ps, ws, z, d_model, tm, tn):
    n = z.shape[0]
    assert n % tm == 0 and d_model % tn == 0
    nj = d_model // tn
    pspec = pl.BlockSpec((tm, W_A), lambda i, j, l: (i, 0))
    wspec = pl.BlockSpec((None, W_A, tn), lambda i, j, l: (l[0], 0, j))
    assert OFF_G % tn == 0
    gspec = lambda k: pl.BlockSpec((tm, tn), lambda i, j, l: (i, OFF_G // tn + k * nj + j))
    return pl.pallas_call(
        _merge_kernel,
        grid_spec=pltpu.PrefetchScalarGridSpec(
            num_scalar_prefetch=1,
            grid=(n // tm, nj),
            in_specs=[pspec] * 4 + [wspec] * 4 + [gspec(k) for k in range(N_BRANCH)],
            out_specs=pl.BlockSpec((tm, tn), lambda i, j, l: (i, j))),
        out_shape=jax.ShapeDtypeStruct((n, d_model), BF16),
        compiler_params=_cparams("arbitrary", "arbitrary"),
        name="merge",
    )(layer, *ps, *ws, z, z, z, z)


def _oproj_kernel(l_ref, x_ref, m_ref, w_ref, o_ref):
    o_ref[...] = x_ref[...] + _dot(m_ref[...], w_ref[...])


def _oproj(layer, x, merged, w, tm, tn):
    n, d = x.shape
    assert n % tm == 0 and d % tn == 0
    return pl.pallas_call(
        _oproj_kernel,
        grid_spec=pltpu.PrefetchScalarGridSpec(
            num_scalar_prefetch=1,
            grid=(n // tm, d // tn),
            in_specs=[pl.BlockSpec((tm, tn), lambda i, j, l: (i, j)),
                      pl.BlockSpec((tm, d), lambda i, j, l: (i, 0)),
                      pl.BlockSpec((None, d, tn), lambda i, j, l: (l[0], 0, j))],
            out_specs=pl.BlockSpec((tm, tn), lambda i, j, l: (i, j))),
        out_shape=jax.ShapeDtypeStruct((n, d), F32),
        compiler_params=_cparams("arbitrary", "arbitrary"),
        name="oproj",
    )(layer, x, merged, w)


def _mlp_kernel(l_ref, x_ref, g_ref, w1_ref, w2_ref, o_ref, h_ref):
    @pl.when(pl.program_id(1) == 0)
    def _():
        _rmsnorm_to(x_ref, g_ref, h_ref)
        o_ref[...] = x_ref[...]
    a = jnp.square(jnp.maximum(_dot(h_ref[...], w1_ref[...]), 0.0)).astype(BF16)
    o_ref[...] += _dot(a, w2_ref[...])


def _mlp(layer, x, g, w1, w2, tm, tf):
    n, d = x.shape
    d_ff = w1.shape[-1]
    assert n % tm == 0 and d_ff % tf == 0
    return pl.pallas_call(
        _mlp_kernel,
        grid_spec=pltpu.PrefetchScalarGridSpec(
            num_scalar_prefetch=1,
            grid=(n // tm, d_ff // tf),
            in_specs=[pl.BlockSpec((tm, d), lambda i, f, l: (i, 0)),
                      pl.BlockSpec((1, d), lambda i, f, l: (0, 0)),
                      pl.BlockSpec((None, d, tf), lambda i, f, l: (l[0], 0, f)),
                      pl.BlockSpec((None, tf, d), lambda i, f, l: (l[0], f, 0))],
            out_specs=pl.BlockSpec((tm, d), lambda i, f, l: (i, 0)),
            scratch_shapes=[pltpu.VMEM((tm, d), BF16)]),
        out_shape=jax.ShapeDtypeStruct((n, d), F32),
        compiler_params=_cparams("arbitrary", "arbitrary"),
        name="mlp",
    )(layer, x, g, w1, w2)


def _tiles(n, seq):
    return dict(tm_mm=min(1024, n), tm_branch=min(512, seq))


def kernel(x, rel_bias, norm_mix_g, w_in, conv_a_w, w_out_a, conv_b_w, conv_b_bias, ln_b_g, ln_b_b,
           w_out_b, ln_c_g, ln_c_b, w_spatial, b_spatial, w_out_c, q_norm_g, k_norm_g, w_out_d, w_o,
           norm_mlp_g, w_mlp_in, w_mlp_out):
    bsz, seq, d_model = x.shape
    n = bsz * seq
    assert seq % MOBA_BLOCK == 0 and w_in.shape[-1] == OFF_G + N_BRANCH * d_model
    t = _tiles(n, seq)
    tm = t["tm_mm"]
    row = lambda a: a[:, None, :]

    w_in, w_o, w_mlp_in, w_mlp_out = (a.astype(BF16) for a in (w_in, w_o, w_mlp_in, w_mlp_out))
    w_outs = tuple(a.astype(BF16) for a in (w_out_a, w_out_b, w_out_c, w_out_d))
    small = dict(
        norm_mix_g=row(norm_mix_g),
        conv_a_w=conv_a_w, conv_b_w=conv_b_w, conv_b_bias=row(conv_b_bias),
        ln_b_g=row(ln_b_g), ln_b_b=row(ln_b_b), ln_c_g=row(ln_c_g), ln_c_b=row(ln_c_b),
        w_spatial=w_spatial,
        b_spatial2=jnp.repeat(jnp.swapaxes(b_spatial, 1, 2), W_C // GMLP_GROUPS, axis=2),
        q_norm_g=row(q_norm_g), k_norm_g=row(k_norm_g), norm_mlp_g=row(norm_mlp_g),
    )
    bias_t = _bias_tiles(rel_bias, seq // MOBA_BLOCK)

    def layer(xc, l, lw):
        li = jnp.full((1,), l, jnp.int32)
        z = _inproj(li, xc, lw["norm_mix_g"], w_in, 0, w_in.shape[-1], BF16, tm, 1024)
        qat, ka, vt = _qkprep(z, lw["q_norm_g"], lw["k_norm_g"], bsz, seq)
        pd = _attn(qat, ka, vt, bias_t, bsz, seq)
        pa, pb, pc = _branch(z, lw, bsz, seq, t["tm_branch"])
        merged = _merge(li, (pa, pb, pc, pd), w_outs, z, d_model, tm, 512)
        xc = _oproj(li, xc, merged, w_o, tm, 1024)
        return _mlp(li, xc, lw["norm_mlp_g"], w_mlp_in, w_mlp_out, tm, 512)

    depth = w_in.shape[0]
    xc = layer(x.reshape(n, d_model), 0, {k: v[0] for k, v in small.items()})
    if depth > 1:
        rest = dict(l=jnp.arange(1, depth, dtype=jnp.int32), lw={k: v[1:] for k, v in small.items()})
        xc, _ = lax.scan(lambda c, xs: (layer(c, xs["l"], xs["lw"]), None), xc, rest)
    return xc.reshape(bsz, seq, d_model)
```

```python
import functools
import math

import jax
import jax.numpy as jnp
from jax import lax
from jax.experimental import pallas as pl
from jax.experimental.pallas import tpu as pltpu

F32 = jnp.float32
BF16 = jnp.bfloat16

W_A = 512
CONV_A = 3
W_B = 512
CONV_B = 31
W_C = 512
GMLP_CHUNK = 128
GMLP_GROUPS = 4
N_HEADS = 4
HEAD_DIM = 128
W_D = N_HEADS * HEAD_DIM
MOBA_BLOCK = 256
MOBA_TOPK = 3
N_BUCKETS = 32
REL_MAX_DIST = 2048
N_BRANCH = 4
EPS = 1e-6
NEG_INF = -1e30
BELOW_NEG_INF = -3e38

OFF_A = 0
OFF_B = OFF_A + 3 * W_A
OFF_C = OFF_B + 2 * W_B
OFF_D = OFF_C + 2 * W_C
OFF_G = OFF_D + 3 * W_D

LANES = 128
SUBLANES = 8
VMEM_LIMIT_BYTES = 56 * 1024 * 1024
HALO_A = 8
HALO_B = 32
CONV_ROWS = 32
KV_BLOCKS = 8
Q_BLOCKS = 4
V_ROWS = HEAD_DIM + 16
LOG2E = math.log2(math.e)


def _cparams(*semantics):
    return pltpu.CompilerParams(dimension_semantics=semantics,
                                vmem_limit_bytes=VMEM_LIMIT_BYTES)


def _dot(a, b):
    return jnp.dot(a, b, preferred_element_type=F32)


def _dot_nt(a, b, precision=None):
    return lax.dot_general(a, b, (((1,), (1,)), ((), ())), precision=precision,
                           preferred_element_type=F32)


def _rms(x, g):
    return x * lax.rsqrt(jnp.mean(x * x, axis=-1, keepdims=True) + EPS) * g


def _layer_norm(x, g, b):
    mu = jnp.mean(x, axis=-1, keepdims=True)
    xc = x - mu
    return xc * lax.rsqrt(jnp.mean(xc * xc, axis=-1, keepdims=True) + EPS) * g + b


def _rmsnorm_to(x_ref, g_ref, h_ref, rows=128):
    def body(c, carry):
        r = pl.multiple_of(c * rows, rows)
        h_ref[pl.ds(r, rows), :] = _rms(x_ref[pl.ds(r, rows), :], g_ref[...]).astype(h_ref.dtype)
        return carry
    lax.fori_loop(0, x_ref.shape[0] // rows, body, 0)


def _inproj_kernel(l_ref, x_ref, g_ref, w_ref, o_ref, h_ref):
    @pl.when(pl.program_id(1) == 0)
    def _():
        _rmsnorm_to(x_ref, g_ref, h_ref)
    o_ref[...] = _dot(h_ref[...], w_ref[...]).astype(o_ref.dtype)


def _inproj(layer, x, g, w, col_lo, ncols, out_dtype, tm, tn):
    n, d = x.shape
    assert n % tm == 0 and ncols % tn == 0 and col_lo % tn == 0
    c0 = col_lo // tn
    return pl.pallas_call(
        _inproj_kernel,
        grid_spec=pltpu.PrefetchScalarGridSpec(
            num_scalar_prefetch=1,
            grid=(n // tm, ncols // tn),
            in_specs=[pl.BlockSpec((tm, d), lambda i, j, l: (i, 0)),
                      pl.BlockSpec((1, d), lambda i, j, l: (0, 0)),
                      pl.BlockSpec((None, d, tn), lambda i, j, l: (l[0], 0, c0 + j))],
            out_specs=pl.BlockSpec((tm, tn), lambda i, j, l: (i, j)),
            scratch_shapes=[pltpu.VMEM((tm, d), BF16)]),
        out_shape=jax.ShapeDtypeStruct((n, ncols), out_dtype),
        compiler_params=_cparams("arbitrary", "arbitrary"),
        name="inproj",
    )(layer, x, g, w)


def _bias_tile_kernel(rb_ref, o_ref):
    h = pl.program_id(0)
    dlt = pl.program_id(1)
    c = lax.broadcasted_iota(jnp.int32, (MOBA_BLOCK, MOBA_BLOCK), 0)
    r = lax.broadcasted_iota(jnp.int32, (MOBA_BLOCK, MOBA_BLOCK), 1)
    d = dlt * MOBA_BLOCK + r - c
    n = jnp.maximum(d, 0)
    max_exact = N_BUCKETS // 2
    nf = jnp.maximum(n, 1).astype(F32)
    large = max_exact + (jnp.log(nf / max_exact) / math.log(REL_MAX_DIST / max_exact)
                         * (N_BUCKETS - max_exact)).astype(jnp.int32)
    large = jnp.minimum(large, N_BUCKETS - 1)
    bucket = jnp.where(n < max_exact, n, large)
    val = jnp.zeros((MOBA_BLOCK, MOBA_BLOCK), F32)
    for b in range(N_BUCKETS):
        val = jnp.where(bucket == b, rb_ref[b, h], val)
    o_ref[0, 0] = jnp.where(d >= 0, val * LOG2E, NEG_INF)


def _bias_tiles(rel_bias, n_blk):
    return pl.pallas_call(
        _bias_tile_kernel,
        grid=(N_HEADS, n_blk),
        in_specs=[pl.BlockSpec(memory_space=pltpu.SMEM)],
        out_specs=pl.BlockSpec((1, 1, MOBA_BLOCK, MOBA_BLOCK), lambda h, d: (h, d, 0, 0)),
        out_shape=jax.ShapeDtypeStruct((N_HEADS, n_blk, MOBA_BLOCK, MOBA_BLOCK), F32),
        compiler_params=_cparams("arbitrary", "arbitrary"),
        name="bias_tiles",
    )(rel_bias)


def _qkprep_kernel(q_ref, k_ref, v_ref, gq_ref, gk_ref, qat_ref, ka_ref, vt_ref, kmean_ref):
    i = pl.program_id(1)

    @pl.when(i == 0)
    def _():
        kmean_ref[...] = jnp.zeros_like(kmean_ref)

    col = lax.broadcasted_iota(jnp.int32, (MOBA_BLOCK, LANES), 1)
    onehot = jnp.where(col == i, 1.0, 0.0).astype(BF16)
    slot = lax.broadcasted_iota(jnp.int32, (LANES, MOBA_BLOCK), 0)
    past = slot < i
    scale = HEAD_DIM ** -0.5
    for h in range(N_HEADS):
        sl = slice(h * HEAD_DIM, (h + 1) * HEAD_DIM)
        lo = 2 * h * HEAD_DIM
        qnt = _rms(q_ref[:, sl].astype(F32), gq_ref[...]).T
        kn = _rms(k_ref[:, sl].astype(F32), gk_ref[...])
        gate = jnp.dot(kmean_ref[h], qnt, precision=lax.Precision.HIGHEST,
                       preferred_element_type=F32)
        g = jnp.where(past, gate, NEG_INF)
        sel = slot < 0
        for _ in range(MOBA_TOPK):
            m = jnp.max(g, axis=0, keepdims=True)
            idx = jnp.min(jnp.where(g == m, slot, LANES), axis=0, keepdims=True)
            hit = slot == idx
            sel = jnp.logical_or(sel, hit)
            g = jnp.where(hit, BELOW_NEG_INF, g)
        keep = jnp.logical_or(jnp.logical_and(sel, past), slot == i)
        qat_ref[lo:lo + HEAD_DIM, :] = (qnt * (scale * LOG2E)).astype(BF16)
        qat_ref[lo + HEAD_DIM:lo + 2 * HEAD_DIM, :] = jnp.where(keep, 0.0, NEG_INF).astype(BF16)
        ka_ref[:, lo:lo + HEAD_DIM] = kn.astype(BF16)
        ka_ref[:, lo + HEAD_DIM:lo + 2 * HEAD_DIM] = onehot
        kmean_ref[h, pl.ds(i, 1), :] = jnp.mean(kn, axis=0, keepdims=True)
        vt_ref[h * V_ROWS:h * V_ROWS + HEAD_DIM, :] = v_ref[:, sl].astype(F32).T.astype(BF16)
        vt_ref[h * V_ROWS + HEAD_DIM:(h + 1) * V_ROWS, :] = jnp.ones((V_ROWS - HEAD_DIM, MOBA_BLOCK), BF16)


def _qkprep(zf, gq, gk, bsz, seq):
    n = bsz * seq
    n_blk = seq // MOBA_BLOCK
    assert n_blk <= LANES
    cq = OFF_D // W_D
    row = lambda b, i: b * n_blk + i
    return pl.pallas_call(
        _qkprep_kernel,
        grid=(bsz, n_blk),
        in_specs=[pl.BlockSpec((MOBA_BLOCK, W_D), lambda b, i: (row(b, i), cq)),
                  pl.BlockSpec((MOBA_BLOCK, W_D), lambda b, i: (row(b, i), cq + 1)),
                  pl.BlockSpec((MOBA_BLOCK, W_D), lambda b, i: (row(b, i), cq + 2)),
                  pl.BlockSpec((1, HEAD_DIM), lambda b, i: (0, 0)),
                  pl.BlockSpec((1, HEAD_DIM), lambda b, i: (0, 0))],
        out_specs=[pl.BlockSpec((2 * W_D, MOBA_BLOCK), lambda b, i: (0, row(b, i))),
                   pl.BlockSpec((MOBA_BLOCK, 2 * W_D), lambda b, i: (row(b, i), 0)),
                   pl.BlockSpec((N_HEADS * V_ROWS, MOBA_BLOCK), lambda b, i: (0, row(b, i)))],
        out_shape=[jax.ShapeDtypeStruct((2 * W_D, n), BF16),
                   jax.ShapeDtypeStruct((n, 2 * W_D), BF16),
                   jax.ShapeDtypeStruct((N_HEADS * V_ROWS, n), BF16)],
        scratch_shapes=[pltpu.VMEM((N_HEADS, LANES, HEAD_DIM), F32)],
        compiler_params=_cparams("arbitrary", "arbitrary"),
        name="qkprep",
    )(zf, zf, zf, gq, gk)


def _attn_kernel(qat_ref, ka_ref, vt_ref, bias_ref, o_ref, s_ref):
    t = pl.program_id(2)
    n_groups = (Q_BLOCKS * t) // KV_BLOCKS + 1
    kv = KV_BLOCKS * MOBA_BLOCK

    def scores(n, c, buf):
        mx = None
        for g in range(n):
            s = _dot(ka_ref[g * kv:(g + 1) * kv, :],
                     qat_ref[:, c * MOBA_BLOCK:(c + 1) * MOBA_BLOCK])
            for jj in range(KV_BLOCKS):
                blk = g * KV_BLOCKS + jj
                dlt = jnp.maximum(Q_BLOCKS * t + c - blk, 0)
                part = s[jj * MOBA_BLOCK:(jj + 1) * MOBA_BLOCK, :] + bias_ref[0, dlt]
                s_ref[buf, blk * MOBA_BLOCK:(blk + 1) * MOBA_BLOCK, :] = part
                pm = jnp.max(part, axis=0, keepdims=True)
                mx = pm if mx is None else jnp.maximum(mx, pm)
        return mx

    def values(n, c, buf, m):
        pv = None
        for blk in range(n * KV_BLOCKS):
            rows = slice(blk * MOBA_BLOCK, (blk + 1) * MOBA_BLOCK)
            p = jnp.exp2(s_ref[buf, rows, :] - m)
            d = _dot(vt_ref[:, rows], p.astype(BF16))
            pv = d if pv is None else pv + d
        o_ref[c * MOBA_BLOCK:(c + 1) * MOBA_BLOCK, :] = (
            pv[:HEAD_DIM, :] / pv[HEAD_DIM:HEAD_DIM + 1, :]).T.astype(o_ref.dtype)

    def run(n):
        m = scores(n, 0, 0)
        for c in range(Q_BLOCKS):
            m_next = scores(n, c + 1, (c + 1) % 2) if c + 1 < Q_BLOCKS else None
            values(n, c, c % 2, m)
            m = m_next

    for n in range(1, ka_ref.shape[0] // kv + 1):
        pl.when(n_groups == n)(functools.partial(run, n))


def _attn(qat, ka, vt, bias_t, bsz, seq):
    n = bsz * seq
    n_blk = seq // MOBA_BLOCK
    assert n_blk % KV_BLOCKS == 0 and KV_BLOCKS % Q_BLOCKS == 0
    tq = Q_BLOCKS * MOBA_BLOCK
    nt = n_blk // Q_BLOCKS
    return pl.pallas_call(
        _attn_kernel,
        grid=(N_HEADS, bsz, nt),
        in_specs=[pl.BlockSpec((2 * HEAD_DIM, tq), lambda h, b, t: (h, b * nt + t)),
                  pl.BlockSpec((seq, 2 * HEAD_DIM), lambda h, b, t: (b, h)),
                  pl.BlockSpec((V_ROWS, seq), lambda h, b, t: (h, b)),
                  pl.BlockSpec((1, n_blk, MOBA_BLOCK, MOBA_BLOCK), lambda h, b, t: (h, 0, 0, 0))],
        out_specs=pl.BlockSpec((tq, HEAD_DIM), lambda h, b, t: (b * nt + t, h)),
        out_shape=jax.ShapeDtypeStruct((n, W_D), BF16),
        scratch_shapes=[pltpu.VMEM((2, seq, MOBA_BLOCK), F32)],
        compiler_params=_cparams("arbitrary", "arbitrary", "arbitrary"),
        name="attn",
    )(qat, ka, vt, bias_t)


def _branch_kernel(ab_ref, ac_ref, ax_ref, ba_ref, bg_ref, cu_ref, cv_ref,
                   wa_ref, wb_ref, bb_ref, lbg_ref, lbb_ref, lcg_ref, lcb_ref, ws_ref, bs_ref,
                   pa_ref, pb_ref, pc_ref, exta_ref, extb_ref):
    tm = ab_ref.shape[0]
    f32 = lambda ref, *idx: ref[idx if idx else ...].astype(F32)

    @pl.when(pl.program_id(1) == 0)
    def _():
        exta_ref[0:HALO_A, :] = jnp.zeros((HALO_A, W_A), F32)
        extb_ref[0, 0:HALO_B, :] = jnp.zeros((HALO_B, W_B), F32)

    exta_ref[HALO_A:HALO_A + tm, :] = f32(ac_ref) * f32(ax_ref)
    extb_ref[0, HALO_B:HALO_B + tm, :] = f32(ba_ref) * jax.nn.sigmoid(f32(bg_ref))
    shifted_rows = tm + HALO_B - SUBLANES
    for s in range(1, SUBLANES):
        extb_ref[s, 0:shifted_rows, :] = extb_ref[0, s:s + shifted_rows, :]

    for r in range(0, tm, CONV_ROWS):
        ya = None
        for k in range(CONV_A):
            off = r + HALO_A - (CONV_A - 1) + k
            term = wa_ref[k:k + 1, :] * exta_ref[off:off + CONV_ROWS, :]
            ya = term if ya is None else ya + term
        pa_ref[r:r + CONV_ROWS, :] = (f32(ab_ref, slice(r, r + CONV_ROWS), slice(None)) * ya).astype(pa_ref.dtype)

        yb = None
        for k in range(CONV_B):
            a, s = divmod(HALO_B - (CONV_B - 1) + k, SUBLANES)
            off = r + a * SUBLANES
            term = wb_ref[k:k + 1, :] * extb_ref[s, off:off + CONV_ROWS, :]
            yb = term if yb is None else yb + term
        hb = _layer_norm(yb + bb_ref[...], lbg_ref[...], lbb_ref[...])
        pb_ref[r:r + CONV_ROWS, :] = jax.nn.silu(hb).astype(pb_ref.dtype)

    exta_ref[0:HALO_A, :] = exta_ref[tm:tm + HALO_A, :]
    extb_ref[0, 0:HALO_B, :] = extb_ref[0, tm:tm + HALO_B, :]

    t_idx = lax.broadcasted_iota(jnp.int32, (GMLP_CHUNK, GMLP_CHUNK), 0)
    s_idx = lax.broadcasted_iota(jnp.int32, (GMLP_CHUNK, GMLP_CHUNK), 1)
    wg = W_C // GMLP_GROUPS
    wm = [jnp.where(s_idx <= t_idx, ws_ref[g], 0.0).astype(BF16) for g in range(GMLP_GROUPS)]
    for r in range(0, tm, GMLP_CHUNK):
        rows = slice(r, r + GMLP_CHUNK)
        u = jax.nn.gelu(f32(cu_ref, rows, slice(None)))
        vn = _layer_norm(jax.nn.gelu(f32(cv_ref, rows, slice(None))), lcg_ref[...], lcb_ref[...]).astype(BF16)
        for g in range(GMLP_GROUPS):
            cols = slice(g * wg, (g + 1) * wg)
            sv = _dot(wm[g], vn[:, cols]) + bs_ref[:, cols]
            pc_ref[rows, cols] = (u[:, cols] * sv).astype(pc_ref.dtype)


def _branch(zf, lw, bsz, seq, tm):
    n = bsz * seq
    assert seq % tm == 0 and tm % GMLP_CHUNK == 0 and tm % CONV_ROWS == 0
    nt = seq // tm
    zspec = lambda c: pl.BlockSpec((tm, W_A), lambda b, t: (b * nt + t, c))
    full = lambda a: pl.BlockSpec(a.shape, lambda b, t: (0,) * a.ndim)
    params = [lw["conv_a_w"], lw["conv_b_w"], lw["conv_b_bias"], lw["ln_b_g"], lw["ln_b_b"],
              lw["ln_c_g"], lw["ln_c_b"], lw["w_spatial"], lw["b_spatial2"]]
    ospec = pl.BlockSpec((tm, W_A), lambda b, t: (b * nt + t, 0))
    return pl.pallas_call(
        _branch_kernel,
        grid=(bsz, nt),
        in_specs=[zspec(c) for c in range(7)] + [full(a) for a in params],
        out_specs=[ospec, ospec, ospec],
        out_shape=[jax.ShapeDtypeStruct((n, W_A), BF16)] * 3,
        scratch_shapes=[pltpu.VMEM((tm + HALO_A, W_A), F32),
                        pltpu.VMEM((SUBLANES, tm + HALO_B, W_B), F32)],
        compiler_params=_cparams("arbitrary", "arbitrary"),
        name="branch",
    )(*([zf] * 7), *params)


def _merge_kernel(l_ref, pa_ref, pb_ref, pc_ref, pd_ref, wa_ref, wb_ref, wc_ref, wd_ref,
                  ga_ref, gb_ref, gc_ref, gd_ref, o_ref):
    acc = None
    for p_ref, w_ref, g_ref in ((pa_ref, wa_ref, ga_ref), (pb_ref, wb_ref, gb_ref),
                                (pc_ref, wc_ref, gc_ref), (pd_ref, wd_ref, gd_ref)):
        gate = 0.5 * jnp.tanh(0.5 * g_ref[...].astype(F32)) + 0.5
        y = gate * _dot(p_ref[...], w_ref[...])
        acc = y if acc is None else acc + y
    o_ref[...] = acc.astype(o_ref.dtype)


def _merge(layer, ps, ws, z, d_model, tm, tn):
    n = z.shape[0]
    assert n % tm == 0 and d_model % tn == 0
    nj = d_model // tn
    pspec = pl.BlockSpec((tm, W_A), lambda i, j, l: (i, 0))
    wspec = pl.BlockSpec((None, W_A, tn), lambda i, j, l: (l[0], 0, j))
    assert OFF_G % tn == 0
    gspec = lambda k: pl.BlockSpec((tm, tn), lambda i, j, l: (i, OFF_G // tn + k * nj + j))
    return pl.pallas_call(
        _merge_kernel,
        grid_spec=pltpu.PrefetchScalarGridSpec(
            num_scalar_prefetch=1,
            grid=(n // tm, nj),
            in_specs=[pspec] * 4 + [wspec] * 4 + [gspec(k) for k in range(N_BRANCH)],
            out_specs=pl.BlockSpec((tm, tn), lambda i, j, l: (i, j))),
        out_shape=jax.ShapeDtypeStruct((n, d_model), BF16),
        compiler_params=_cparams("arbitrary", "arbitrary"),
        name="merge",
    )(layer, *ps, *ws, z, z, z, z)


def _oproj_kernel(l_ref, x_ref, m_ref, w_ref, o_ref):
    o_ref[...] = x_ref[...] + _dot(m_ref[...], w_ref[...])


def _oproj(layer, x, merged, w, tm, tn):
    n, d = x.shape
    assert n % tm == 0 and d % tn == 0
    return pl.pallas_call(
        _oproj_kernel,
        grid_spec=pltpu.PrefetchScalarGridSpec(
            num_scalar_prefetch=1,
            grid=(n // tm, d // tn),
            in_specs=[pl.BlockSpec((tm, tn), lambda i, j, l: (i, j)),
                      pl.BlockSpec((tm, d), lambda i, j, l: (i, 0)),
                      pl.BlockSpec((None, d, tn), lambda i, j, l: (l[0], 0, j))],
            out_specs=pl.BlockSpec((tm, tn), lambda i, j, l: (i, j))),
        out_shape=jax.ShapeDtypeStruct((n, d), F32),
        compiler_params=_cparams("arbitrary", "arbitrary"),
        name="oproj",
    )(layer, x, merged, w)


def _mlp_kernel(l_ref, x_ref, g_ref, w1_ref, w2_ref, o_ref, h_ref):
    @pl.when(pl.program_id(1) == 0)
    def _():
        _rmsnorm_to(x_ref, g_ref, h_ref)
        o_ref[...] = x_ref[...]
    a = jnp.square(jnp.maximum(_dot(h_ref[...], w1_ref[...]), 0.0)).astype(BF16)
    o_ref[...] += _dot(a, w2_ref[...])


def _mlp(layer, x, g, w1, w2, tm, tf):
    n, d = x.shape
    d_ff = w1.shape[-1]
    assert n % tm == 0 and d_ff % tf == 0
    return pl.pallas_call(
        _mlp_kernel,
        grid_spec=pltpu.PrefetchScalarGridSpec(
            num_scalar_prefetch=1,
            grid=(n // tm, d_ff // tf),
            in_specs=[pl.BlockSpec((tm, d), lambda i, f, l: (i, 0)),
                      pl.BlockSpec((1, d), lambda i, f, l: (0, 0)),
                      pl.BlockSpec((None, d, tf), lambda i, f, l: (l[0], 0, f)),
                      pl.BlockSpec((None, tf, d), lambda i, f, l: (l[0], f, 0))],
            out_specs=pl.BlockSpec((tm, d), lambda i, f, l: (i, 0)),
            scratch_shapes=[pltpu.VMEM((tm, d), BF16)]),
        out_shape=jax.ShapeDtypeStruct((n, d), F32),
        compiler_params=_cparams("arbitrary", "arbitrary"),
        name="mlp",
    )(layer, x, g, w1, w2)


def _tiles(n, seq):
    return dict(tm_mm=min(1024, n), tm_branch=min(512, seq))


def kernel(x, rel_bias, norm_mix_g, w_in, conv_a_w, w_out_a, conv_b_w, conv_b_bias, ln_b_g, ln_b_b,
           w_out_b, ln_c_g, ln_c_b, w_spatial, b_spatial, w_out_c, q_norm_g, k_norm_g, w_out_d, w_o,
           norm_mlp_g, w_mlp_in, w_mlp_out):
    bsz, seq, d_model = x.shape
    n = bsz * seq
    assert seq % MOBA_BLOCK == 0 and w_in.shape[-1] == OFF_G + N_BRANCH * d_model
    t = _tiles(n, seq)
    tm = t["tm_mm"]
    row = lambda a: a[:, None, :]

    w_in, w_o, w_mlp_in, w_mlp_out = (a.astype(BF16) for a in (w_in, w_o, w_mlp_in, w_mlp_out))
    w_outs = tuple(a.astype(BF16) for a in (w_out_a, w_out_b, w_out_c, w_out_d))
    small = dict(
        norm_mix_g=row(norm_mix_g),
        conv_a_w=conv_a_w, conv_b_w=conv_b_w, conv_b_bias=row(conv_b_bias),
        ln_b_g=row(ln_b_g), ln_b_b=row(ln_b_b), ln_c_g=row(ln_c_g), ln_c_b=row(ln_c_b),
        w_spatial=w_spatial,
        b_spatial2=jnp.repeat(jnp.swapaxes(b_spatial, 1, 2), W_C // GMLP_GROUPS, axis=2),
        q_norm_g=row(q_norm_g), k_norm_g=row(k_norm_g), norm_mlp_g=row(norm_mlp_g),
    )
    bias_t = _bias_tiles(rel_bias, seq // MOBA_BLOCK)

    def layer(xc, l, lw):
        li = jnp.full((1,), l, jnp.int32)
        z = _inproj(li, xc, lw["norm_mix_g"], w_in, 0, w_in.shape[-1], BF16, tm, 1024)
        qat, ka, vt = _qkprep(z, lw["q_norm_g"], lw["k_norm_g"], bsz, seq)
        pd = _attn(qat, ka, vt, bias_t, bsz, seq)
        pa, pb, pc = _branch(z, lw, bsz, seq, t["tm_branch"])
        merged = _merge(li, (pa, pb, pc, pd), w_outs, z, d_model, tm, 1024)
        xc = _oproj(li, xc, merged, w_o, tm, 1024)
        return _mlp(li, xc, lw["norm_mlp_g"], w_mlp_in, w_mlp_out, tm, 512)

    depth = w_in.shape[0]
    xc = layer(x.reshape(n, d_model), 0, {k: v[0] for k, v in small.items()})
    if depth > 1:
        rest = dict(l=jnp.arange(1, depth, dtype=jnp.int32), lw={k: v[1:] for k, v in small.items()})
        xc, _ = lax.scan(lambda c, xs: (layer(c, xs["l"], xs["lw"]), None), xc, rest)
    return xc.reshape(bsz, seq, d_model)
```
